```python
import math
import jax, jax.numpy as jnp
from jax import lax
import numpy as np

D_MODEL = 1024
BATCH = 2
SEQ = 8192
DEPTH = 1

GRID_W = 64
CTX_LEN = 256
GLA_HEADS = 4
GLA_DK = 64
GLA_DV = 128
GLA_RANK = 16
GLA_NORMALIZER = 16.0
GLA_CHUNK = 64
ATT_HEADS = 8
ATT_KV_HEADS = 2
ATT_DH = 64
WINDOW = 128
ATT_BLOCK = 128
ROPE_THETA = 10000.0
PEER_HEADS = 8
N_KEYS = 128
N_EXPERTS = N_KEYS * N_KEYS
PEER_DQ = 128
PEER_TOPK = 16
PEER_BLOCK = 128
GLA_WIDTH = GLA_HEADS * GLA_DV
ATT_WIDTH = ATT_HEADS * ATT_DH
MIX_WIDTH = GLA_WIDTH + ATT_WIDTH
N_MOD = 6
LN_EPS = 1e-5
NEG_INF = -1e30
SPLIT_SIZES = (GLA_HEADS * GLA_DK, GLA_HEADS * GLA_DK, GLA_WIDTH, GLA_WIDTH, GLA_RANK, GLA_RANK,
               ATT_WIDTH, ATT_KV_HEADS * ATT_DH, ATT_KV_HEADS * ATT_DH)
SPLIT_POINTS = tuple(sum(SPLIT_SIZES[:i + 1]) for i in range(len(SPLIT_SIZES) - 1))
IN_WIDTH = sum(SPLIT_SIZES)

kernel_name = "hybrid_gla_swa_peer_dit_block"


def _layernorm(x, g, b):
    xf = x.astype(jnp.float32)
    mu = jnp.mean(xf, axis=-1, keepdims=True)
    var = jnp.mean(jnp.square(xf - mu), axis=-1, keepdims=True)
    y = (xf - mu) * lax.rsqrt(var + LN_EPS) * g.astype(jnp.float32) + b.astype(jnp.float32)
    return y.astype(x.dtype)


def _axial_rope(n_tokens):
    rows = n_tokens // GRID_W
    row = jnp.repeat(jnp.arange(rows, dtype=jnp.float32), GRID_W)
    col = jnp.tile(jnp.arange(GRID_W, dtype=jnp.float32), rows)
    n_freq = ATT_DH // 4
    inv = ROPE_THETA ** (-jnp.arange(n_freq, dtype=jnp.float32) / n_freq)
    ang = jnp.stack([row[:, None] * inv, col[:, None] * inv], axis=1)
    return jnp.cos(ang), jnp.sin(ang)


def _apply_rope(x, cos, sin):
    B, T, H, _ = x.shape
    xr = x.astype(jnp.float32).reshape(B, T, H, 2, 2, ATT_DH // 4)
    x1, x2 = xr[..., 0, :], xr[..., 1, :]
    cs, sn = cos[None, :, None], sin[None, :, None]
    out = jnp.stack([x1 * cs - x2 * sn, x2 * cs + x1 * sn], axis=-2)
    return out.reshape(B, T, H, ATT_DH).astype(x.dtype)


def _gla_prepare(parts, w_gate2_f, b_gate_f, w_gate2_b, b_gate_b):
    q, k, v, r, zf, zb = parts[:6]
    B, T = q.shape[:2]

    def heads(a, d):
        return a.reshape(B, T, GLA_HEADS, d).transpose(0, 2, 1, 3).astype(jnp.float32)

    def decay(z, w2, b):
        return heads(jax.nn.log_sigmoid((z @ w2 + b).astype(jnp.float32)) / GLA_NORMALIZER, GLA_DK)

    return (heads(q, GLA_DK) * (GLA_DK ** -0.5), heads(k, GLA_DK), heads(v, GLA_DV),
            decay(zf, w_gate2_f, b_gate_f), decay(zb, w_gate2_b, b_gate_b), r)


def _gla_chunked(q, k, v, g, s0):
    B, H, T, dk = q.shape
    dv = v.shape[-1]
    n = T // GLA_CHUNK

    def to_chunks(a):
        return a.reshape(B, H, n, GLA_CHUNK, a.shape[-1]).transpose(2, 0, 1, 3, 4)

    tril = jnp.tril(jnp.ones((GLA_CHUNK, GLA_CHUNK), dtype=bool))

    def step(S, inp):
        qi, ki, vi, gi = inp
        b = jnp.cumsum(gi, axis=-2)
        qe = qi * jnp.exp(b)
        ke = ki * jnp.exp(-b)
        a = jnp.where(tril, jnp.einsum('bhcd,bhsd->bhcs', qe, ke), 0.0)
        o = jnp.einsum('bhcs,bhsv->bhcv', a, vi) + jnp.einsum('bhcd,bhdv->bhcv', qe, S)
        b_last = b[..., -1:, :]
        kd = ki * jnp.exp(b_last - b)
        S_new = jnp.exp(b_last)[..., 0, :, None] * S + jnp.einsum('bhsd,bhsv->bhdv', kd, vi)
        return S_new, o

    S, o = lax.scan(step, s0, (to_chunks(q), to_chunks(k), to_chunks(v), to_chunks(g)))
    return o.transpose(1, 2, 0, 3, 4).reshape(B, H, T, dv), S


def _gla_bidirectional(q, k, v, g_f, g_b, s0_f, s0_b):
    o_f, s_f = _gla_chunked(q, k, v, g_f, s0_f)
    flip = lambda a: jnp.flip(a, axis=2)
    o_b, s_b = _gla_chunked(flip(q), flip(k), flip(v), flip(g_b), s0_b)
    return o_f + flip(o_b), s_f, s_b


def _gla_output(o, r, gla_norm_g):
    B, H, T, dv = o.shape
    o = o * lax.rsqrt(jnp.mean(jnp.square(o), axis=-1, keepdims=True) + LN_EPS) * gla_norm_g.astype(jnp.float32)
    o = o.transpose(0, 2, 1, 3).reshape(B, T, H * dv)
    return o.astype(r.dtype) * jax.nn.silu(r)


def _window_attention(q, k, v, kc, vc, sink):
    B, T, H, dh = q.shape
    L = kc.shape[1]
    KV = ATT_KV_HEADS
    G = H // KV
    BLK = ATT_BLOCK
    nb = T // BLK
    scale = dh ** -0.5
    qb = q.reshape(B, nb, BLK, KV, G, dh)

    def band(a):
        ap = jnp.pad(a, ((0, 0), (BLK, BLK), (0, 0), (0, 0)))
        return jnp.concatenate([ap[:, o:o + T].reshape(B, nb, BLK, KV, dh) for o in (0, BLK, 2 * BLK)], axis=2)

    kb, vb = band(k), band(v)
    qi = jnp.arange(BLK)[:, None]
    sj = jnp.arange(3 * BLK)[None, :]
    key_pos = jnp.arange(nb)[:, None, None] * BLK - BLK + sj[None]
    valid = (jnp.abs(sj - BLK - qi)[None] <= WINDOW) & (key_pos >= 0) & (key_pos < T)
    s_loc = jnp.einsum('bnqkgd,bnskd->bnkgqs', qb, kb).astype(jnp.float32) * scale
    s_loc = jnp.where(valid[None, :, None, None], s_loc, NEG_INF)
    s_ctx = jnp.einsum('bnqkgd,blkd->bnkgql', qb, kc).astype(jnp.float32) * scale
    s_sink = jnp.broadcast_to(sink.astype(jnp.float32).reshape(KV, G)[None, None, :, :, None, None],
                              s_loc.shape[:-1] + (1,))
    p = jax.nn.softmax(jnp.concatenate([s_loc, s_ctx, s_sink], axis=-1), axis=-1)
    p_loc = p[..., :3 * BLK].astype(v.dtype)
    p_ctx = p[..., 3 * BLK:3 * BLK + L].astype(v.dtype)
    o = jnp.einsum('bnkgqs,bnskd->bnqkgd', p_loc, vb) + jnp.einsum('bnkgql,blkd->bnqkgd', p_ctx, vc)
    return o.reshape(B, T, H * dh)


def _ctx_attention(qc, kc, vc, sink):
    B, L, H, dh = qc.shape
    KV = ATT_KV_HEADS
    G = H // KV
    qg = qc.reshape(B, L, KV, G, dh)
    s = jnp.einsum('blkgd,bmkd->bkglm', qg, kc).astype(jnp.float32) * (dh ** -0.5)
    s_sink = jnp.broadcast_to(sink.astype(jnp.float32).reshape(KV, G)[None, :, :, None, None], s.shape[:-1] + (1,))
    p = jax.nn.softmax(jnp.concatenate([s, s_sink], axis=-1), axis=-1)[..., :L]
    o = jnp.einsum('bkglm,bmkd->blkgd', p.astype(vc.dtype), vc)
    return o.reshape(B, L, H * dh)


def _mixer(h, hc, w_in, w_gate2_f, b_gate_f, w_gate2_b, b_gate_b, gla_norm_g, attn_sink, w_out,
           cos, sin, with_ctx_out):
    B, T, _ = h.shape
    L = hc.shape[1]
    lat = jnp.split(h @ w_in, SPLIT_POINTS, axis=-1)
    con = jnp.split(hc @ w_in, SPLIT_POINTS, axis=-1)
    ql, kl, vl, gfl, gbl, rl = _gla_prepare(lat, w_gate2_f, b_gate_f, w_gate2_b, b_gate_b)
    qc, kc, vc, gfc, gbc, rc = _gla_prepare(con, w_gate2_f, b_gate_f, w_gate2_b, b_gate_b)
    zeros = jnp.zeros((B, GLA_HEADS, GLA_DK, GLA_DV), jnp.float32)
    o_c, s_f, s_b = _gla_bidirectional(qc, kc, vc, gfc, gbc, zeros, zeros)
    o_l, _, _ = _gla_bidirectional(ql, kl, vl, gfl, gbl, s_f, s_b)
    gla_lat = _gla_output(o_l, rl, gla_norm_g)
    aq = _apply_rope(lat[6].reshape(B, T, ATT_HEADS, ATT_DH), cos, sin)
    ak = _apply_rope(lat[7].reshape(B, T, ATT_KV_HEADS, ATT_DH), cos, sin)
    av = lat[8].reshape(B, T, ATT_KV_HEADS, ATT_DH)
    cq = con[6].reshape(B, L, ATT_HEADS, ATT_DH)
    ck = con[7].reshape(B, L, ATT_KV_HEADS, ATT_DH)
    cv = con[8].reshape(B, L, ATT_KV_HEADS, ATT_DH)
    att_lat = _window_attention(aq, ak, av, ck, cv, attn_sink)
    y_lat = jnp.concatenate([gla_lat, att_lat], axis=-1) @ w_out
    y_ctx = None
    if with_ctx_out:
        gla_ctx = _gla_output(o_c, rc, gla_norm_g)
        att_ctx = _ctx_attention(cq, ck, cv, attn_sink)
        y_ctx = jnp.concatenate([gla_ctx, att_ctx], axis=-1) @ w_out
    return y_lat, y_ctx


def _peer(h, peer_wq, peer_subkeys, peer_u, peer_v):
    B, T, D = h.shape
    xb = h.reshape(-1, PEER_BLOCK, D)

    def block(xt):
        P = xt.shape[0]
        qh = (xt @ peer_wq).reshape(P, PEER_HEADS, 2, PEER_DQ // 2)
        s = jnp.einsum('phad,hand->phan', qh, peer_subkeys).astype(jnp.float32)
        s1, i1 = lax.top_k(s[:, :, 0], PEER_TOPK)
        s2, i2 = lax.top_k(s[:, :, 1], PEER_TOPK)
        cand = (s1[..., :, None] + s2[..., None, :]).reshape(P, PEER_HEADS, PEER_TOPK * PEER_TOPK)
        cidx = (i1[..., :, None] * N_KEYS + i2[..., None, :]).reshape(P, PEER_HEADS, PEER_TOPK * PEER_TOPK)
        top_s, pos = lax.top_k(cand, PEER_TOPK)
        eidx = jnp.take_along_axis(cidx, pos, axis=-1)
        gate = jax.nn.softmax(top_s, axis=-1)
        u = peer_u[eidx]
        v = peer_v[eidx]
        act = jax.nn.gelu(jnp.einsum('pd,phkd->phk', xt, u).astype(jnp.float32), approximate=False)
        return jnp.einsum('phk,phkd->pd', (gate * act).astype(v.dtype), v)

    return lax.map(block, xb).reshape(B, T, D)


def setup_inputs(seed: int = 0) -> dict:
    key = jax.random.key(seed)
    ks = jax.random.split(key, 24)
    D = D_MODEL
    beta = (8.0 * DEPTH) ** -0.25
    nrm = lambda k, shape: jax.random.normal(k, shape, jnp.float32)
    col_scale = jnp.ones((IN_WIDTH,), jnp.float32)
    col_scale = col_scale.at[SPLIT_POINTS[1]:SPLIT_POINTS[2]].set(beta)
    col_scale = col_scale.at[SPLIT_POINTS[7]:].set(beta)
    return {
        "x": nrm(ks[0], (BATCH, SEQ, D)),
        "c": nrm(ks[1], (BATCH, D)),
        "ctx": nrm(ks[2], (BATCH, CTX_LEN, D)),
        "c_ctx": nrm(ks[3], (D,)),
        "w_ada": nrm(ks[4], (DEPTH, D, N_MOD * D)) * D ** -0.5,
        "b_ada": nrm(ks[5], (DEPTH, N_MOD * D)) * 0.02,
        "w_in": nrm(ks[6], (DEPTH, D, IN_WIDTH)) * D ** -0.5 * col_scale,
        "w_gate2_f": nrm(ks[7], (DEPTH, GLA_RANK, GLA_HEADS * GLA_DK)) * GLA_RANK ** -0.5,
        "b_gate_f": nrm(ks[8], (DEPTH, GLA_HEADS * GLA_DK)) * 0.02,
        "w_gate2_b": nrm(ks[9], (DEPTH, GLA_RANK, GLA_HEADS * GLA_DK)) * GLA_RANK ** -0.5,
        "b_gate_b": nrm(ks[10], (DEPTH, GLA_HEADS * GLA_DK)) * 0.02,
        "gla_norm_g": 1.0 + 0.01 * nrm(ks[11], (DEPTH, GLA_DV)),
        "attn_sink": nrm(ks[12], (DEPTH, ATT_HEADS)),
        "w_out": nrm(ks[13], (DEPTH, MIX_WIDTH, D)) * MIX_WIDTH ** -0.5 * beta,
        "ln1_g": 1.0 + 0.01 * nrm(ks[14], (DEPTH, D)),
        "ln1_b": 0.01 * nrm(ks[15], (DEPTH, D)),
        "peer_wq": nrm(ks[16], (DEPTH, D, PEER_HEADS * PEER_DQ)) * D ** -0.5,
        "peer_subkeys": nrm(ks[17], (DEPTH, PEER_HEADS, 2, N_KEYS, PEER_DQ // 2)) * (PEER_DQ // 2) ** -0.5,
        "peer_u": nrm(ks[18], (DEPTH, N_EXPERTS, D)) * D ** -0.5,
        "peer_v": nrm(ks[19], (DEPTH, N_EXPERTS, D)) * 0.5 * beta,
        "ln2_g": 1.0 + 0.01 * nrm(ks[20], (DEPTH, D)),
        "ln2_b": 0.01 * nrm(ks[21], (DEPTH, D)),
    }


def reference(x, c, ctx, c_ctx, w_ada, b_ada, w_in, w_gate2_f, b_gate_f, w_gate2_b, b_gate_b,
              gla_norm_g, attn_sink, w_out, ln1_g, ln1_b, peer_wq, peer_subkeys, peer_u, peer_v,
              ln2_g, ln2_b):
    B, T, D = x.shape
    alpha = (2.0 * DEPTH) ** 0.25
    cos, sin = _axial_rope(T)
    silu_c = jax.nn.silu(c)
    silu_cc = jax.nn.silu(c_ctx)
    for i in range(DEPTH):
        last = i == DEPTH - 1
        mod = (silu_c @ w_ada[i] + b_ada[i]).reshape(B, N_MOD, 1, D)
        modc = (silu_cc @ w_ada[i] + b_ada[i]).reshape(N_MOD, D)
        sh1, sc1, g1, sh2, sc2, g2 = [mod[:, j] for j in range(N_MOD)]
        csh1, csc1, cg1, csh2, csc2, cg2 = [modc[j] for j in range(N_MOD)]
        h = x * (1.0 + sc1) + sh1
        hc = ctx * (1.0 + csc1) + csh1
        y, yc = _mixer(h, hc, w_in[i], w_gate2_f[i], b_gate_f[i], w_gate2_b[i], b_gate_b[i],
                       gla_norm_g[i], attn_sink[i], w_out[i], cos, sin, not last)
        x = _layernorm(alpha * x + g1 * y, ln1_g[i], ln1_b[i])
        x = _layernorm(alpha * x + g2 * _peer(x * (1.0 + sc2) + sh2, peer_wq[i], peer_subkeys[i],
                                              peer_u[i], peer_v[i]), ln2_g[i], ln2_b[i])
        if not last:
            ctx = _layernorm(alpha * ctx + cg1 * yc, ln1_g[i], ln1_b[i])
            ctx = _layernorm(alpha * ctx + cg2 * _peer(ctx * (1.0 + csc2) + csh2, peer_wq[i], peer_subkeys[i],
                                                       peer_u[i], peer_v[i]), ln2_g[i], ln2_b[i])
    return x
```

```python
import functools

import jax
import jax.numpy as jnp
from jax import lax
from jax.experimental import pallas as pl
from jax.experimental.pallas import tpu as pltpu

F32 = jnp.float32
BF16 = jnp.bfloat16

GRID_W = 64
GLA_HEADS = 4
GLA_DK = 64
GLA_DV = 128
GLA_RANK = 16
GLA_NORMALIZER = 16.0
GLA_CHUNK = 64
ATT_HEADS = 8
ATT_KV_HEADS = 2
ATT_DH = 64
WINDOW = 128
ATT_BLOCK = 128
ROPE_THETA = 10000.0
PEER_HEADS = 8
N_KEYS = 128
PEER_DQ = 128
PEER_TOPK = 16
N_MOD = 6
LN_EPS = 1e-5
NEG_INF = -1e30
LANES = 128

VMEM_LIMIT = 56 * 1024 * 1024

_NT = (((1,), (1,)), ((), ()))
_TN = (((0,), (0,)), ((), ()))


def _params(sem):
    return pltpu.CompilerParams(dimension_semantics=sem, vmem_limit_bytes=VMEM_LIMIT)


def _ada_kernel(s_ref, w_ref, b_ref, o_ref, *, n_rows):
    tn = w_ref.shape[1]
    w = w_ref[...]
    rows = []
    for r in range(n_rows):
        s = s_ref[r]
        cols = [jnp.sum(w[:, j * LANES:(j + 1) * LANES] * s, axis=0, keepdims=True)
                for j in range(tn // LANES)]
        rows.append(jnp.concatenate(cols, axis=1) + b_ref[...])
    rows.append(jnp.zeros((8 - n_rows, tn), F32))
    o_ref[...] = jnp.concatenate(rows, axis=0)


def _ada(cond_bc, w, b):
    n_rows, d, _ = cond_bc.shape
    n = w.shape[1]
    tn = 512
    return pl.pallas_call(
        functools.partial(_ada_kernel, n_rows=n_rows),
        grid=(n // tn,),
        in_specs=[pl.BlockSpec((n_rows, d, LANES), lambda j: (0, 0, 0)),
                  pl.BlockSpec((d, tn), lambda j: (0, j)),
                  pl.BlockSpec((1, tn), lambda j: (0, j))],
        out_specs=pl.BlockSpec((8, tn), lambda j: (0, j)),
        out_shape=jax.ShapeDtypeStruct((8, n), F32),
        compiler_params=_params(("parallel",)),
        name="ada",
    )(cond_bc, w, b)


def _rope(a, cos, sin):
    n = a.shape[1] // LANES
    c = jnp.concatenate([cos] * n, axis=1)
    s = jnp.concatenate([sin] * n, axis=1)
    lane = lax.broadcasted_iota(jnp.int32, a.shape, 1)
    swapped = jnp.where((lane % 32) < 16,
                        pltpu.roll(a, a.shape[1] - 16, 1),
                        pltpu.roll(a, 16, 1))
    return a * c + swapped * s


def _inproj_kernel(x_ref, sc_ref, sh_ref, w_ref, w2_ref, bg_ref, cos_ref, sin_ref,
                   gq_ref, gk_ref, gv_ref, r_ref, g_ref, aq_ref, ak_ref, av_ref):
    h = x_ref[0] * (1.0 + sc_ref[0]) + sh_ref[0]
    y = jnp.dot(h.astype(BF16), w_ref[...], preferred_element_type=F32)
    gq_ref[0] = y[:, 0:256] * (GLA_DK ** -0.5)
    gk_ref[0] = y[:, 256:512]
    gv_ref[0] = y[:, 512:1024].astype(BF16)
    r_ref[0] = y[:, 1024:1536]
    cos = cos_ref[...]
    sin = sin_ref[...]
    aq_ref[0] = (_rope(y[:, 1536:2048], cos, sin) * (ATT_DH ** -0.5)).astype(BF16)
    ak_ref[0] = _rope(y[:, 2048:2304], cos, sin).astype(BF16)
    av_ref[0] = y[:, 2304:2560].astype(BF16)
    z = y[:, 2560:2688]
    logits = jnp.dot(z.astype(BF16), w2_ref[...], preferred_element_type=F32) + bg_ref[...]
    g = (jnp.minimum(logits, 0.0) - jnp.log1p(jnp.exp(-jnp.abs(logits)))) * (1.0 / GLA_NORMALIZER)
    g_ref[0, 0] = g[:, 0:256]
    g_ref[1, 0] = g[:, 256:512]


def _inproj(x, mod3, sc_row, sh_row, w_p, w2, bg, cos, sin, tm):
    B, T, D = x.shape
    NP = w_p.shape[1]
    tok = lambda w, dt: jax.ShapeDtypeStruct((B, T, w), dt)
    tspec = lambda w: pl.BlockSpec((1, tm, w), lambda b, t: (b, t, 0))
    return pl.pallas_call(
        _inproj_kernel,
        grid=(B, T // tm),
        in_specs=[tspec(D),
                  pl.BlockSpec((1, 1, D), lambda b, t: (sc_row(b), 0, 0)),
                  pl.BlockSpec((1, 1, D), lambda b, t: (sh_row(b), 0, 0)),
                  pl.BlockSpec((D, NP), lambda b, t: (0, 0)),
                  pl.BlockSpec((LANES, 512), lambda b, t: (0, 0)),
                  pl.BlockSpec((1, 512), lambda b, t: (0, 0)),
                  pl.BlockSpec((tm, LANES), lambda b, t: (t, 0)),
                  pl.BlockSpec((tm, LANES), lambda b, t: (t, 0))],
        out_specs=[tspec(256), tspec(256), tspec(512), tspec(512),
                   pl.BlockSpec((2, 1, tm, 256), lambda b, t: (0, b, t, 0)),
                   tspec(512), tspec(256), tspec(256)],
        out_shape=[tok(256, F32), tok(256, F32), tok(512, BF16), tok(512, F32),
                   jax.ShapeDtypeStruct((2, B, T, 256), F32),
                   tok(512, BF16), tok(256, BF16), tok(256, BF16)],
        compiler_params=_params(("parallel", "parallel")),
        name="inproj",
    )(x, mod3, mod3, w_p, w2, bg, cos, sin)


def _gla_kernel(q_ref, k_ref, g_ref, v_ref, ck_ref, cg_ref, cv_ref, o_ref, st_ref,
                *, n_chunks, n_ctx_chunks):
    d = pl.program_id(2)
    t = pl.program_id(3)
    C = GLA_CHUNK
    fwd = d == 0
    ri = lax.broadcasted_iota(jnp.int32, (C, C), 0)
    ci = lax.broadcasted_iota(jnp.int32, (C, C), 1)
    tri = jnp.where(fwd, ri - ci, ci - ri) >= 0
    tri_f = tri.astype(F32)
    tri2 = jnp.concatenate([tri, tri], axis=0)
    lane = lax.broadcasted_iota(jnp.int32, (1, LANES), 1)
    m0 = (lane < GLA_DK).astype(F32)
    m1 = 1.0 - m0
    rr = lax.broadcasted_iota(jnp.int32, (2 * GLA_DV, LANES), 0)
    ll = lax.broadcasted_iota(jnp.int32, (2 * GLA_DV, LANES), 1)
    blockdiag = (rr < GLA_DV) == (ll < GLA_DK)

    def chunk(kc, gc, vc, qc):
        b = jnp.dot(tri_f, gc, precision=lax.Precision.HIGHEST, preferred_element_type=F32)
        btot = jnp.sum(gc, axis=0, keepdims=True)
        st = st_ref[...]
        out = None
        if qc is not None:
            qe = qc * jnp.exp(b)
            ke = (kc * jnp.exp(-b)).astype(BF16)
            lhs = jnp.concatenate([qe * m0, qe * m1], axis=0).astype(BF16)
            a = lax.dot_general(lhs, ke, _NT, preferred_element_type=F32)
            a = jnp.where(tri2, a, 0.0).astype(BF16)
            oi = jnp.dot(a, vc, preferred_element_type=F32)
            o_intra = jnp.concatenate([oi[:C, :GLA_DV], oi[C:, GLA_DV:]], axis=1)
            o_inter = lax.dot_general(qe.astype(BF16), st.astype(BF16), _NT,
                                      preferred_element_type=F32)
            out = o_intra + o_inter
        kd = (kc * jnp.exp(btot - b)).astype(BF16)
        mt = lax.dot_general(vc, kd, _TN, preferred_element_type=F32)
        st_ref[...] = st * jnp.exp(btot) + jnp.where(blockdiag, mt, 0.0)
        return out

    @pl.when(t == 0)
    def _():
        st_ref[...] = jnp.zeros_like(st_ref)

        def cbody(j, carry):
            cj = jnp.where(fwd, j, n_ctx_chunks - 1 - j)
            sl = pl.ds(pl.multiple_of(cj * C, C), C)
            chunk(ck_ref[0, sl, :], cg_ref[0, 0, sl, :], cv_ref[0, sl, :], None)
            return carry

        lax.fori_loop(0, n_ctx_chunks, cbody, 0)

    def body(j, carry):
        cj = jnp.where(fwd, j, n_chunks - 1 - j)
        sl = pl.ds(pl.multiple_of(cj * C, C), C)
        o_ref[0, 0, sl, :] = chunk(k_ref[0, sl, :], g_ref[0, 0, sl, :], v_ref[0, sl, :], q_ref[0, sl, :])
        return carry

    lax.fori_loop(0, n_chunks, body, 0)


def _gla(gq, gk, g, gv, ck, cg, cv, tt):
    B, T, _ = gq.shape
    L = ck.shape[1]
    nT = T // tt
    npairs = GLA_HEADS // 2
    teff = lambda d, t: t + d * (nT - 1 - 2 * t)
    return pl.pallas_call(
        functools.partial(_gla_kernel, n_chunks=tt // GLA_CHUNK, n_ctx_chunks=L // GLA_CHUNK),
        grid=(B, npairs, 2, nT),
        in_specs=[pl.BlockSpec((1, tt, LANES), lambda b, p, d, t: (b, teff(d, t), p)),
                  pl.BlockSpec((1, tt, LANES), lambda b, p, d, t: (b, teff(d, t), p)),
                  pl.BlockSpec((1, 1, tt, LANES), lambda b, p, d, t: (d, b, teff(d, t), p)),
                  pl.BlockSpec((1, tt, 2 * GLA_DV), lambda b, p, d, t: (b, teff(d, t), p)),
                  pl.BlockSpec((1, L, LANES), lambda b, p, d, t: (b, 0, p)),
                  pl.BlockSpec((1, 1, L, LANES), lambda b, p, d, t: (d, b, 0, p)),
                  pl.BlockSpec((1, L, 2 * GLA_DV), lambda b, p, d, t: (b, 0, p))],
        out_specs=pl.BlockSpec((1, 1, tt, 2 * GLA_DV), lambda b, p, d, t: (d, b, teff(d, t), p)),
        out_shape=jax.ShapeDtypeStruct((2, B, T, GLA_HEADS * GLA_DV), F32),
        scratch_shapes=[pltpu.VMEM((2 * GLA_DV, LANES), F32)],
        compiler_params=_params(("parallel", "parallel", "arbitrary", "arbitrary")),
        name="gla",
    )(gq, gk, g, gv, ck, cg, cv)


def _attn_kernel(q_ref, kp_ref, kc_ref, kn_ref, vp_ref, vc_ref, vn_ref, ck_ref, cv_ref, sink_ref,
                 o_ref, *, n_blocks):
    n = pl.program_id(1)
    BLK = ATT_BLOCK
    L = ck_ref.shape[1]
    nk = 3 * BLK + L
    si = lax.broadcasted_iota(jnp.int32, (BLK, nk), 1)
    qi = lax.broadcasted_iota(jnp.int32, (BLK, nk), 0)
    band = jnp.abs(si - BLK - qi) <= WINDOW
    prev_ok = jnp.logical_or(si >= BLK, n > 0)
    next_ok = jnp.logical_or(si < 2 * BLK, n < n_blocks - 1)
    valid = jnp.logical_or(si >= 3 * BLK, band & prev_ok & next_ok)
    lane = lax.broadcasted_iota(jnp.int32, (1, LANES), 1)
    half = [(lane < ATT_DH).astype(F32), (lane >= ATT_DH).astype(F32)]
    group = ATT_HEADS // ATT_KV_HEADS
    for gi in range(ATT_HEADS // 2):
        kv = (2 * gi) // group
        ks = slice(kv * LANES, (kv + 1) * LANES)
        kcat = jnp.concatenate([kp_ref[0, :, ks], kc_ref[0, :, ks], kn_ref[0, :, ks], ck_ref[0, :, ks]], axis=0)
        vcat = jnp.concatenate([vp_ref[0, :, ks], vc_ref[0, :, ks], vn_ref[0, :, ks], cv_ref[0, :, ks]], axis=0)
        q2 = q_ref[0, :, gi * LANES:(gi + 1) * LANES].astype(F32)
        outs = []
        for hh in range(2):
            h = 2 * gi + hh
            qm = (q2 * half[hh]).astype(BF16)
            s = lax.dot_general(qm, kcat, _NT, preferred_element_type=F32)
            s = jnp.where(valid, s, NEG_INF)
            snk = sink_ref[h:h + 1, 0:1]
            m = jnp.maximum(jnp.max(s, axis=1, keepdims=True), snk)
            p = jnp.exp(s - m)
            den = jnp.sum(p, axis=1, keepdims=True) + jnp.exp(snk - m)
            o = jnp.dot(p.astype(BF16), vcat, preferred_element_type=F32)
            outs.append(o / den)
        o_ref[0, :, gi * LANES:(gi + 1) * LANES] = jnp.where(lane < ATT_DH, outs[0], outs[1]).astype(o_ref.dtype)


def _attn(aq, ak, av, ck, cv, sink_bc):
    B, T, _ = aq.shape
    L = ck.shape[1]
    nb = T // ATT_BLOCK
    kvw = ak.shape[2]
    cur = lambda b, n: (b, n, 0)
    prev = lambda b, n: (b, jnp.maximum(n - 1, 0), 0)
    nxt = lambda b, n: (b, jnp.minimum(n + 1, nb - 1), 0)
    kspec = lambda im: pl.BlockSpec((1, ATT_BLOCK, kvw), im)
    return pl.pallas_call(
        functools.partial(_attn_kernel, n_blocks=nb),
        grid=(B, nb),
        in_specs=[pl.BlockSpec((1, ATT_BLOCK, ATT_HEADS * ATT_DH), cur),
                  kspec(prev), kspec(cur), kspec(nxt), kspec(prev), kspec(cur), kspec(nxt),
                  pl.BlockSpec((1, L, kvw), lambda b, n: (b, 0, 0)),
                  pl.BlockSpec((1, L, kvw), lambda b, n: (b, 0, 0)),
                  pl.BlockSpec((ATT_HEADS, LANES), lambda b, n: (0, 0))],
        out_specs=pl.BlockSpec((1, ATT_BLOCK, ATT_HEADS * ATT_DH), cur),
        out_shape=jax.ShapeDtypeStruct((B, T, ATT_HEADS * ATT_DH), BF16),
        compiler_params=_params(("parallel", "parallel")),
        name="attn",
    )(aq, ak, ak, ak, av, av, av, ck, cv, sink_bc)


def _layernorm(v, g, b):
    mu = jnp.mean(v, axis=-1, keepdims=True)
    vc = v - mu
    var = jnp.mean(vc * vc, axis=-1, keepdims=True)
    return vc * lax.rsqrt(var + LN_EPS) * g + b


def _outproj_kernel(of_ref, ob_ref, r_ref, att_ref, x_ref, g1_ref, sh2_ref, sc2_ref,
                    wout_ref, gng_ref, lng_ref, lnb_ref, wqt_ref, skh_ref, skl_ref,
                    x1_ref, hpt_ref, st_ref, *, alpha):
    o = of_ref[0, 0] + ob_ref[0, 0]
    r = r_ref[0]
    gng = gng_ref[...]
    parts = []
    for h in range(GLA_HEADS):
        oh = o[:, h * GLA_DV:(h + 1) * GLA_DV]
        oh = oh * lax.rsqrt(jnp.mean(oh * oh, axis=-1, keepdims=True) + LN_EPS) * gng
        rh = r[:, h * GLA_DV:(h + 1) * GLA_DV]
        parts.append((oh * (rh * jax.nn.sigmoid(rh))).astype(BF16))
    parts.append(att_ref[0])
    cat = jnp.concatenate(parts, axis=1)
    y = jnp.dot(cat, wout_ref[...], preferred_element_type=F32)
    x1 = _layernorm(alpha * x_ref[0] + g1_ref[0] * y, lng_ref[...], lnb_ref[...])
    x1_ref[0] = x1
    hp = x1 * (1.0 + sc2_ref[0]) + sh2_ref[0]
    hpt = hp.T.astype(BF16)
    hpt_ref[0] = hpt
    qt = jnp.dot(wqt_ref[...], hpt, preferred_element_type=F32)
    for h in range(PEER_HEADS):
        qh = qt[h * PEER_DQ:(h + 1) * PEER_DQ, :]
        qhi = qh.astype(BF16)
        qlo = (qh - qhi.astype(F32)).astype(BF16)
        skh = skh_ref[h]
        s = (jnp.dot(skh, qhi, preferred_element_type=F32)
             + jnp.dot(skh, qlo, preferred_element_type=F32)
             + jnp.dot(skl_ref[h], qhi, preferred_element_type=F32))
        st_ref[0, h] = s


def _outproj(o2, r, att, x, mod3, w_out, gng, ln_g, ln_b, wqt, skh, skl, alpha, tm):
    B, T, D = x.shape
    W = o2.shape[3]
    tok = lambda w: pl.BlockSpec((1, tm, w), lambda b, t: (b, t, 0))
    modrow = lambda j: pl.BlockSpec((1, 1, D), lambda b, t: (b * N_MOD + j, 0, 0))
    full = lambda a: pl.BlockSpec(a.shape, lambda b, t: (0,) * a.ndim)
    return pl.pallas_call(
        functools.partial(_outproj_kernel, alpha=alpha),
        grid=(B, T // tm),
        in_specs=[pl.BlockSpec((1, 1, tm, W), lambda b, t: (0, b, t, 0)),
                  pl.BlockSpec((1, 1, tm, W), lambda b, t: (1, b, t, 0)),
                  tok(W), tok(att.shape[2]), tok(D), modrow(2), modrow(3), modrow(4),
                  full(w_out), full(gng), full(ln_g), full(ln_b), full(wqt), full(skh), full(skl)],
        out_specs=[tok(D),
                   pl.BlockSpec((1, D, tm), lambda b, t: (b, 0, t)),
                   pl.BlockSpec((1, PEER_HEADS, 2 * N_KEYS, tm), lambda b, t: (b, 0, 0, t))],
        out_shape=[jax.ShapeDtypeStruct((B, T, D), F32),
                   jax.ShapeDtypeStruct((B, D, T), BF16),
                   jax.ShapeDtypeStruct((B, PEER_HEADS, 2 * N_KEYS, T), F32)],
        compiler_params=_params(("parallel", "parallel")),
        name="outproj",
    )(o2, o2, r, att, x, mod3, mod3, mod3, w_out, gng, ln_g, ln_b, wqt, skh, skl)


def _extract_desc(v, count):
    vals = []
    for _ in range(count):
        m = jnp.max(v, axis=0, keepdims=True)
        vals.append(m)
        v = jnp.where(v == m, NEG_INF, v)
    return vals


def _topk_kernel(s_ref, th_ref, cc_ref, e2_ref):
    K = PEER_TOPK
    s1 = s_ref[0, 0, 0:N_KEYS, :]
    s2 = s_ref[0, 0, N_KEYS:2 * N_KEYS, :]
    a = _extract_desc(s1, K)
    b = _extract_desc(s2, K)
    bmat = jnp.concatenate(b, axis=0)
    row8 = lax.broadcasted_iota(jnp.int32, (8, s1.shape[1]), 0)
    cands = [a[0] + bmat]
    for r in range(1, K // 2):
        cands.append(jnp.where(row8 < K // (r + 1), a[r] + bmat[0:8], NEG_INF))
    cands.append(jnp.concatenate(a[K // 2:], axis=0) + b[0])
    cand = jnp.concatenate(cands, axis=0)
    top = _extract_desc(cand, K + 1)
    tau = 0.5 * (top[K - 1] + top[K])
    mx = top[0]
    zsum = top[0] * 0.0
    for r in range(K):
        zsum = zsum + jnp.exp(top[r] - mx)
    th_ref[0, 0] = tau - s1
    cc_ref[0, 0] = jnp.exp(s1 - a[0]) / zsum
    e2_ref[0, 0] = jnp.exp(s2 - b[0])


def _topk(st, tp):
    B, H, _, T = st.shape
    out = jax.ShapeDtypeStruct((B, H, N_KEYS, T), F32)
    ospec = pl.BlockSpec((1, 1, N_KEYS, tp), lambda b, h, t: (b, h, 0, t))
    return pl.pallas_call(
        _topk_kernel,
        grid=(B, H, T // tp),
        in_specs=[pl.BlockSpec((1, 1, 2 * N_KEYS, tp), lambda b, h, t: (b, h, 0, t))],
        out_specs=[ospec, ospec, ospec],
        out_shape=[out, out, out],
        compiler_params=_params(("parallel", "parallel", "parallel")),
        name="topk",
    )(st)


def _peer_kernel(hpt_ref, u_ref, vt_ref, s2_ref, e2_ref, th_ref, cc_ref, x1_ref, g2_ref,
                 lng_ref, lnb_ref, o_ref, acc_ref, z_ref, a_ref, *, alpha, n_steps):
    e = pl.program_id(2)
    te, tm = z_ref.shape

    @pl.when(e == 0)
    def _():
        acc_ref[...] = jnp.zeros_like(acc_ref)

    z_ref[...] = jnp.dot(u_ref[...], hpt_ref[0], preferred_element_type=F32)
    for ii in range(te // N_KEYS):
        rows = slice(ii * N_KEYS, (ii + 1) * N_KEYS)
        for c in range(tm // LANES):
            cols = slice(c * LANES, (c + 1) * LANES)
            w = jnp.zeros((N_KEYS, LANES), F32)
            for h in range(PEER_HEADS):
                th = th_ref[0, h, ii, :, cols]
                cc = cc_ref[0, h, ii, :, cols]
                w = w + jnp.where(s2_ref[0, h, :, cols] >= th, e2_ref[0, h, :, cols] * cc, 0.0)
            z = z_ref[rows, cols]
            act = 0.5 * z * (1.0 + lax.erf(z * (2.0 ** -0.5)))
            a_ref[rows, cols] = (w * act).astype(BF16)
    acc_ref[...] += jnp.dot(vt_ref[...], a_ref[...], preferred_element_type=F32)

    @pl.when(e == n_steps - 1)
    def _():
        y = acc_ref[...].T
        o_ref[0] = _layernorm(alpha * x1_ref[0] + g2_ref[0] * y, lng_ref[...], lnb_ref[...])


def _peer(hpt, u, vt, st, e2, th, cc, x1, mod3, ln_g, ln_b, alpha, tm, te):
    B, D, T = hpt.shape
    E = u.shape[0]
    n_steps = E // te
    side = lambda blk: pl.BlockSpec((1, PEER_HEADS, N_KEYS, tm), lambda b, t, e: (b, 0, blk, t))
    th = th.reshape(B, PEER_HEADS, N_KEYS, 1, T)
    cc = cc.reshape(B, PEER_HEADS, N_KEYS, 1, T)
    keyrow = pl.BlockSpec((1, PEER_HEADS, te // N_KEYS, 1, tm), lambda b, t, e: (b, 0, e, 0, t))
    return pl.pallas_call(
        functools.partial(_peer_kernel, alpha=alpha, n_steps=n_steps),
        grid=(B, T // tm, n_steps),
        in_specs=[pl.BlockSpec((1, D, tm), lambda b, t, e: (b, 0, t)),
                  pl.BlockSpec((te, D), lambda b, t, e: (e, 0)),
                  pl.BlockSpec((D, te), lambda b, t, e: (0, e)),
                  side(1), side(0), keyrow, keyrow,
                  pl.BlockSpec((1, tm, D), lambda b, t, e: (b, t, 0)),
                  pl.BlockSpec((1, 1, D), lambda b, t, e: (b * N_MOD + 5, 0, 0)),
                  pl.BlockSpec((1, D), lambda b, t, e: (0, 0)),
                  pl.BlockSpec((1, D), lambda b, t, e: (0, 0))],
        out_specs=pl.BlockSpec((1, tm, D), lambda b, t, e: (b, t, 0)),
        out_shape=jax.ShapeDtypeStruct((B, T, D), F32),
        scratch_shapes=[pltpu.VMEM((D, tm), F32), pltpu.VMEM((te, tm), F32), pltpu.VMEM((te, tm), BF16)],
        compiler_params=_params(("parallel", "parallel", "arbitrary")),
        name="peer",
    )(hpt, u, vt, st, e2, th, cc, x1, mod3, ln_g, ln_b)


def _rope_tables(n_tokens):
    rows = n_tokens // GRID_W
    row = jnp.repeat(jnp.arange(rows, dtype=F32), GRID_W)
    col = jnp.tile(jnp.arange(GRID_W, dtype=F32), rows)
    n_freq = ATT_DH // 4
    inv = ROPE_THETA ** (-jnp.arange(n_freq, dtype=F32) / n_freq)
    ar, ac = row[:, None] * inv, col[:, None] * inv
    cos = jnp.concatenate([jnp.cos(ar), jnp.cos(ar), jnp.cos(ac), jnp.cos(ac)], axis=1)
    sin = jnp.concatenate([-jnp.sin(ar), jnp.sin(ar), -jnp.sin(ac), jnp.sin(ac)], axis=1)
    return jnp.tile(cos, (1, 2)), jnp.tile(sin, (1, 2))


def _dup_heads(w, n_heads, dh):
    return jnp.concatenate([w[:, h * dh:(h + 1) * dh] for h in range(n_heads) for _ in range(2)], axis=1)


def kernel(x, c, ctx, c_ctx, w_ada, b_ada, w_in, w_gate2_f, b_gate_f, w_gate2_b, b_gate_b, gla_norm_g,
           attn_sink, w_out, ln1_g, ln1_b, peer_wq, peer_subkeys, peer_u, peer_v, ln2_g, ln2_b):
    B, T, D = x.shape
    L = ctx.shape[1]
    depth = w_ada.shape[0]
    assert depth == 1, "single-layer block only"
    alpha = (2.0 * depth) ** 0.25

    cond = jnp.concatenate([jax.nn.silu(c), jax.nn.silu(c_ctx)[None]], axis=0)
    cond_bc = jnp.broadcast_to(cond[:, :, None], (B + 1, D, LANES))
    mod = _ada(cond_bc, w_ada[0], b_ada[0][None])
    mod3 = mod.reshape(8 * N_MOD, 1, D)

    wi = w_in[0]
    kq = 2 * GLA_HEADS * GLA_DK + 2 * GLA_HEADS * GLA_DV
    z0, a0 = kq, kq + 2 * GLA_RANK
    k0 = a0 + ATT_HEADS * ATT_DH
    v0 = k0 + ATT_KV_HEADS * ATT_DH
    w_p = jnp.concatenate([
        wi[:, :kq], wi[:, a0:k0],
        _dup_heads(wi[:, k0:v0], ATT_KV_HEADS, ATT_DH),
        _dup_heads(wi[:, v0:], ATT_KV_HEADS, ATT_DH),
        wi[:, z0:a0], jnp.zeros((D, LANES - 2 * GLA_RANK), F32)], axis=1).astype(BF16)
    gw = GLA_HEADS * GLA_DK
    w2 = jnp.zeros((LANES, 2 * gw), F32)
    w2 = w2.at[0:GLA_RANK, 0:gw].set(w_gate2_f[0]).at[GLA_RANK:2 * GLA_RANK, gw:].set(w_gate2_b[0]).astype(BF16)
    bg = jnp.concatenate([b_gate_f[0], b_gate_b[0]])[None]
    cos, sin = _rope_tables(T)
    ones = jnp.ones((L, LANES), F32)

    gq, gk, gv, r, g, aq, ak, av = _inproj(x, mod3, lambda b: b * N_MOD + 1, lambda b: b * N_MOD,
                                           w_p, w2, bg, cos, sin, tm=min(T, 512))
    _, cgk, cgv, _, cg, _, cak, cav = _inproj(ctx, mod3, lambda b: B * N_MOD + 1, lambda b: B * N_MOD,
                                              w_p, w2, bg, ones, 0.0 * ones, tm=L)

    o2 = _gla(gq, gk, g, gv, cgk, cg, cgv, tt=min(T, 512))
    sink_bc = jnp.broadcast_to(attn_sink[0][:, None], (ATT_HEADS, LANES))
    att = _attn(aq, ak, av, cak, cav, sink_bc)

    sk = peer_subkeys[0]
    half = PEER_DQ // 2
    skb = jnp.zeros((PEER_HEADS, 2 * N_KEYS, PEER_DQ), F32)
    skb = skb.at[:, :N_KEYS, :half].set(sk[:, 0]).at[:, N_KEYS:, half:].set(sk[:, 1])
    skh = skb.astype(BF16)
    skl = (skb - skh.astype(F32)).astype(BF16)
    x1, hpt, st = _outproj(o2, r, att, x, mod3, w_out[0].astype(BF16), gla_norm_g[0][None],
                           ln1_g[0][None], ln1_b[0][None], peer_wq[0].T.astype(BF16), skh, skl,
                           alpha, tm=min(T, 256))

    th, cc, e2 = _topk(st, tp=min(T, 256))
    u = peer_u[0].astype(BF16)
    vt = peer_v[0].T.astype(BF16)
    return _peer(hpt, u, vt, st, e2, th, cc, x1, mod3, ln2_g[0][None], ln2_b[0][None],
                 alpha, tm=min(T, 512), te=512)
```

```python
import functools

import jax
import jax.numpy as jnp
from jax import lax
from jax.experimental import pallas as pl
from jax.experimental.pallas import tpu as pltpu

F32 = jnp.float32
BF16 = jnp.bfloat16

GRID_W = 64
GLA_HEADS = 4
GLA_DK = 64
GLA_DV = 128
GLA_RANK = 16
GLA_NORMALIZER = 16.0
GLA_CHUNK = 64
ATT_HEADS = 8
ATT_KV_HEADS = 2
ATT_DH = 64
WINDOW = 128
ATT_BLOCK = 128
ROPE_THETA = 10000.0
PEER_HEADS = 8
N_KEYS = 128
PEER_DQ = 128
PEER_TOPK = 16
N_MOD = 6
LN_EPS = 1e-5
NEG_INF = -1e30
LANES = 128
PEER_CHUNK = 256
LOG2E = 1.4426950408889634
MXU_TILE = 256

VMEM_LIMIT = 56 * 1024 * 1024

_NT = (((1,), (1,)), ((), ()))
_TN = (((0,), (0,)), ((), ()))


def _params(sem):
    return pltpu.CompilerParams(dimension_semantics=sem, vmem_limit_bytes=VMEM_LIMIT)


def _ada_kernel(s_ref, w_ref, b_ref, o_ref, *, n_rows):
    tn = w_ref.shape[1]
    w = w_ref[...]
    rows = []
    for r in range(n_rows):
        s = s_ref[r]
        cols = [jnp.sum(w[:, j * LANES:(j + 1) * LANES] * s, axis=0, keepdims=True)
                for j in range(tn // LANES)]
        rows.append(jnp.concatenate(cols, axis=1) + b_ref[...])
    rows.append(jnp.zeros((8 - n_rows, tn), F32))
    o_ref[...] = jnp.concatenate(rows, axis=0)


def _ada(cond_bc, w, b):
    n_rows, d, _ = cond_bc.shape
    n = w.shape[1]
    tn = 512
    return pl.pallas_call(
        functools.partial(_ada_kernel, n_rows=n_rows),
        grid=(n // tn,),
        in_specs=[pl.BlockSpec((n_rows, d, LANES), lambda j: (0, 0, 0)),
                  pl.BlockSpec((d, tn), lambda j: (0, j)),
                  pl.BlockSpec((1, tn), lambda j: (0, j))],
        out_specs=pl.BlockSpec((8, tn), lambda j: (0, j)),
        out_shape=jax.ShapeDtypeStruct((8, n), F32),
        compiler_params=_params(("parallel",)),
        name="ada",
    )(cond_bc, w, b)


def _rope(a, cos, sin):
    n = a.shape[1] // LANES
    c = jnp.concatenate([cos] * n, axis=1)
    s = jnp.concatenate([sin] * n, axis=1)
    lane = lax.broadcasted_iota(jnp.int32, a.shape, 1)
    swapped = jnp.where((lane % 32) < 16,
                        pltpu.roll(a, a.shape[1] - 16, 1),
                        pltpu.roll(a, 16, 1))
    return a * c + swapped * s


def _inproj_kernel(x_ref, sc_ref, sh_ref, w_ref, w2_ref, bg_ref, cos_ref, sin_ref,
                   gq_ref, gk_ref, gv_ref, r_ref, g_ref, aq_ref, ak_ref, av_ref):
    h = x_ref[0] * (1.0 + sc_ref[0]) + sh_ref[0]
    y = jnp.dot(h.astype(BF16), w_ref[...], preferred_element_type=F32)
    gq_ref[0] = y[:, 0:256] * (GLA_DK ** -0.5)
    gk_ref[0] = y[:, 256:512]
    gv_ref[0] = y[:, 512:1024].astype(BF16)
    r_ref[0] = y[:, 1024:1536]
    cos = cos_ref[...]
    sin = sin_ref[...]
    aq_ref[0] = (_rope(y[:, 1536:2048], cos, sin) * (ATT_DH ** -0.5)).astype(BF16)
    ak_ref[0] = _rope(y[:, 2048:2304], cos, sin).astype(BF16)
    av_ref[0] = y[:, 2304:2560].astype(BF16)
    z = y[:, 2560:2688]
    logits = jnp.dot(z.astype(BF16), w2_ref[...], preferred_element_type=F32) + bg_ref[...]
    g = (jnp.minimum(logits, 0.0) - jnp.log1p(jnp.exp(-jnp.abs(logits)))) * (1.0 / GLA_NORMALIZER)
    g_ref[0, 0] = g[:, 0:256]
    g_ref[1, 0] = g[:, 256:512]


def _inproj(x, mod3, sc_row, sh_row, w_p, w2, bg, cos, sin, tm):
    B, T, D = x.shape
    NP = w_p.shape[1]
    tok = lambda w, dt: jax.ShapeDtypeStruct((B, T, w), dt)
    tspec = lambda w: pl.BlockSpec((1, tm, w), lambda b, t: (b, t, 0))
    return pl.pallas_call(
        _inproj_kernel,
        grid=(B, T // tm),
        in_specs=[tspec(D),
                  pl.BlockSpec((1, 1, D), lambda b, t: (sc_row(b), 0, 0)),
                  pl.BlockSpec((1, 1, D), lambda b, t: (sh_row(b), 0, 0)),
                  pl.BlockSpec((D, NP), lambda b, t: (0, 0)),
                  pl.BlockSpec((LANES, 512), lambda b, t: (0, 0)),
                  pl.BlockSpec((1, 512), lambda b, t: (0, 0)),
                  pl.BlockSpec((tm, LANES), lambda b, t: (t, 0)),
                  pl.BlockSpec((tm, LANES), lambda b, t: (t, 0))],
        out_specs=[tspec(256), tspec(256), tspec(512), tspec(512),
                   pl.BlockSpec((2, 1, tm, 256), lambda b, t: (0, b, t, 0)),
                   tspec(512), tspec(256), tspec(256)],
        out_shape=[tok(256, F32), tok(256, F32), tok(512, BF16), tok(512, F32),
                   jax.ShapeDtypeStruct((2, B, T, 256), F32),
                   tok(512, BF16), tok(256, BF16), tok(256, BF16)],
        compiler_params=_params(("parallel", "parallel")),
        name="inproj",
    )(x, mod3, mod3, w_p, w2, bg, cos, sin)


def _gla_kernel(q_ref, k_ref, g_ref, v_ref, ck_ref, cg_ref, cv_ref, o_ref, st_ref,
                *, n_chunks, n_ctx_chunks):
    d = pl.program_id(2)
    t = pl.program_id(3)
    C = GLA_CHUNK
    fwd = d == 0
    ri = lax.broadcasted_iota(jnp.int32, (C, C), 0)
    ci = lax.broadcasted_iota(jnp.int32, (C, C), 1)
    tri = jnp.where(fwd, ri - ci, ci - ri) >= 0
    tri_f = tri.astype(F32)
    tri2 = jnp.concatenate([tri, tri], axis=0)
    lane = lax.broadcasted_iota(jnp.int32, (1, LANES), 1)
    m0 = (lane < GLA_DK).astype(F32)
    m1 = 1.0 - m0
    rr = lax.broadcasted_iota(jnp.int32, (2 * GLA_DV, LANES), 0)
    ll = lax.broadcasted_iota(jnp.int32, (2 * GLA_DV, LANES), 1)
    blockdiag = (rr < GLA_DV) == (ll < GLA_DK)

    def chunk(kc, gc, vc, qc):
        b = jnp.dot(tri_f, gc, precision=lax.Precision.HIGHEST, preferred_element_type=F32)
        btot = jnp.sum(gc, axis=0, keepdims=True)
        st = st_ref[...]
        out = None
        if qc is not None:
            qe = qc * jnp.exp(b)
            ke = (kc * jnp.exp(-b)).astype(BF16)
            lhs = jnp.concatenate([qe * m0, qe * m1], axis=0).astype(BF16)
            a = lax.dot_general(lhs, ke, _NT, preferred_element_type=F32)
            a = jnp.where(tri2, a, 0.0).astype(BF16)
            oi = jnp.dot(a, vc, preferred_element_type=F32)
            o_intra = jnp.concatenate([oi[:C, :GLA_DV], oi[C:, GLA_DV:]], axis=1)
            o_inter = lax.dot_general(qe.astype(BF16), st.astype(BF16), _NT,
                                      preferred_element_type=F32)
            out = o_intra + o_inter
        kd = (kc * jnp.exp(btot - b)).astype(BF16)
        mt = lax.dot_general(vc, kd, _TN, preferred_element_type=F32)
        st_ref[...] = st * jnp.exp(btot) + jnp.where(blockdiag, mt, 0.0)
        return out

    @pl.when(t == 0)
    def _():
        st_ref[...] = jnp.zeros_like(st_ref)

        def cbody(j, carry):
            cj = jnp.where(fwd, j, n_ctx_chunks - 1 - j)
            sl = pl.ds(pl.multiple_of(cj * C, C), C)
            chunk(ck_ref[0, sl, :], cg_ref[0, 0, sl, :], cv_ref[0, sl, :], None)
            return carry

        lax.fori_loop(0, n_ctx_chunks, cbody, 0)

    def body(j, carry):
        cj = jnp.where(fwd, j, n_chunks - 1 - j)
        sl = pl.ds(pl.multiple_of(cj * C, C), C)
        o_ref[0, 0, sl, :] = chunk(k_ref[0, sl, :], g_ref[0, 0, sl, :], v_ref[0, sl, :], q_ref[0, sl, :])
        return carry

    lax.fori_loop(0, n_chunks, body, 0)


def _gla(gq, gk, g, gv, ck, cg, cv, tt):
    B, T, _ = gq.shape
    L = ck.shape[1]
    nT = T // tt
    npairs = GLA_HEADS // 2
    teff = lambda d, t: t + d * (nT - 1 - 2 * t)
    return pl.pallas_call(
        functools.partial(_gla_kernel, n_chunks=tt // GLA_CHUNK, n_ctx_chunks=L // GLA_CHUNK),
        grid=(B, npairs, 2, nT),
        in_specs=[pl.BlockSpec((1, tt, LANES), lambda b, p, d, t: (b, teff(d, t), p)),
                  pl.BlockSpec((1, tt, LANES), lambda b, p, d, t: (b, teff(d, t), p)),
                  pl.BlockSpec((1, 1, tt, LANES), lambda b, p, d, t: (d, b, teff(d, t), p)),
                  pl.BlockSpec((1, tt, 2 * GLA_DV), lambda b, p, d, t: (b, teff(d, t), p)),
                  pl.BlockSpec((1, L, LANES), lambda b, p, d, t: (b, 0, p)),
                  pl.BlockSpec((1, 1, L, LANES), lambda b, p, d, t: (d, b, 0, p)),
                  pl.BlockSpec((1, L, 2 * GLA_DV), lambda b, p, d, t: (b, 0, p))],
        out_specs=pl.BlockSpec((1, 1, tt, 2 * GLA_DV), lambda b, p, d, t: (d, b, teff(d, t), p)),
        out_shape=jax.ShapeDtypeStruct((2, B, T, GLA_HEADS * GLA_DV), F32),
        scratch_shapes=[pltpu.VMEM((2 * GLA_DV, LANES), F32)],
        compiler_params=_params(("parallel", "parallel", "arbitrary", "arbitrary")),
        name="gla",
    )(gq, gk, g, gv, ck, cg, cv)


def _attn_kernel(q_ref, kp_ref, kc_ref, kn_ref, vp_ref, vc_ref, vn_ref, ck_ref, cv_ref, sink_ref,
                 o_ref, *, n_blocks):
    n = pl.program_id(1)
    BLK = ATT_BLOCK
    L = ck_ref.shape[1]
    nk = 3 * BLK + L
    si = lax.broadcasted_iota(jnp.int32, (BLK, nk), 1)
    qi = lax.broadcasted_iota(jnp.int32, (BLK, nk), 0)
    band = jnp.abs(si - BLK - qi) <= WINDOW
    prev_ok = jnp.logical_or(si >= BLK, n > 0)
    next_ok = jnp.logical_or(si < 2 * BLK, n < n_blocks - 1)
    valid = jnp.logical_or(si >= 3 * BLK, band & prev_ok & next_ok)
    lane = lax.broadcasted_iota(jnp.int32, (1, LANES), 1)
    half = [(lane < ATT_DH).astype(F32), (lane >= ATT_DH).astype(F32)]
    group = ATT_HEADS // ATT_KV_HEADS
    for gi in range(ATT_HEADS // 2):
        kv = (2 * gi) // group
        ks = slice(kv * LANES, (kv + 1) * LANES)
        kcat = jnp.concatenate([kp_ref[0, :, ks], kc_ref[0, :, ks], kn_ref[0, :, ks], ck_ref[0, :, ks]], axis=0)
        vcat = jnp.concatenate([vp_ref[0, :, ks], vc_ref[0, :, ks], vn_ref[0, :, ks], cv_ref[0, :, ks]], axis=0)
        q2 = q_ref[0, :, gi * LANES:(gi + 1) * LANES].astype(F32)
        outs = []
        for hh in range(2):
            h = 2 * gi + hh
            qm = (q2 * half[hh]).astype(BF16)
            s = lax.dot_general(qm, kcat, _NT, preferred_element_type=F32)
            s = jnp.where(valid, s, NEG_INF)
            snk = sink_ref[h:h + 1, 0:1]
            m = jnp.maximum(jnp.max(s, axis=1, keepdims=True), snk)
            p = jnp.exp(s - m)
            den = jnp.sum(p, axis=1, keepdims=True) + jnp.exp(snk - m)
            o = jnp.dot(p.astype(BF16), vcat, preferred_element_type=F32)
            outs.append(o / den)
        o_ref[0, :, gi * LANES:(gi + 1) * LANES] = jnp.where(lane < ATT_DH, outs[0], outs[1]).astype(o_ref.dtype)


def _attn(aq, ak, av, ck, cv, sink_bc):
    B, T, _ = aq.shape
    L = ck.shape[1]
    nb = T // ATT_BLOCK
    kvw = ak.shape[2]
    cur = lambda b, n: (b, n, 0)
    prev = lambda b, n: (b, jnp.maximum(n - 1, 0), 0)
    nxt = lambda b, n: (b, jnp.minimum(n + 1, nb - 1), 0)
    kspec = lambda im: pl.BlockSpec((1, ATT_BLOCK, kvw), im)
    return pl.pallas_call(
        functools.partial(_attn_kernel, n_blocks=nb),
        grid=(B, nb),
        in_specs=[pl.BlockSpec((1, ATT_BLOCK, ATT_HEADS * ATT_DH), cur),
                  kspec(prev), kspec(cur), kspec(nxt), kspec(prev), kspec(cur), kspec(nxt),
                  pl.BlockSpec((1, L, kvw), lambda b, n: (b, 0, 0)),
                  pl.BlockSpec((1, L, kvw), lambda b, n: (b, 0, 0)),
                  pl.BlockSpec((ATT_HEADS, LANES), lambda b, n: (0, 0))],
        out_specs=pl.BlockSpec((1, ATT_BLOCK, ATT_HEADS * ATT_DH), cur),
        out_shape=jax.ShapeDtypeStruct((B, T, ATT_HEADS * ATT_DH), BF16),
        compiler_params=_params(("parallel", "parallel")),
        name="attn",
    )(aq, ak, ak, ak, av, av, av, ck, cv, sink_bc)


def _layernorm(v, g, b):
    mu = jnp.mean(v, axis=-1, keepdims=True)
    vc = v - mu
    var = jnp.mean(vc * vc, axis=-1, keepdims=True)
    return vc * lax.rsqrt(var + LN_EPS) * g + b


def _outproj_kernel(of_ref, ob_ref, r_ref, att_ref, x_ref, g1_ref, sh2_ref, sc2_ref,
                    wout_ref, gng_ref, lng_ref, lnb_ref, wqt_ref, skh_ref, skl_ref,
                    x1_ref, hpt_ref, st_ref, *, alpha):
    o = of_ref[0, 0] + ob_ref[0, 0]
    r = r_ref[0]
    gng = gng_ref[...]
    parts = []
    for h in range(GLA_HEADS):
        oh = o[:, h * GLA_DV:(h + 1) * GLA_DV]
        oh = oh * lax.rsqrt(jnp.mean(oh * oh, axis=-1, keepdims=True) + LN_EPS) * gng
        rh = r[:, h * GLA_DV:(h + 1) * GLA_DV]
        parts.append((oh * (rh * jax.nn.sigmoid(rh))).astype(BF16))
    parts.append(att_ref[0])
    cat = jnp.concatenate(parts, axis=1)
    y = jnp.dot(cat, wout_ref[...], preferred_element_type=F32)
    x1 = _layernorm(alpha * x_ref[0] + g1_ref[0] * y, lng_ref[...], lnb_ref[...])
    x1_ref[0] = x1
    hp = x1 * (1.0 + sc2_ref[0]) + sh2_ref[0]
    hpt = hp.T.astype(BF16)
    hpt_ref[0, 0] = hpt
    qt = jnp.dot(wqt_ref[...], hpt, preferred_element_type=F32)
    for h in range(PEER_HEADS):
        qh = qt[h * PEER_DQ:(h + 1) * PEER_DQ, :]
        qhi = qh.astype(BF16)
        qlo = (qh - qhi.astype(F32)).astype(BF16)
        skh = skh_ref[h]
        s = (jnp.dot(skh, qhi, preferred_element_type=F32)
             + jnp.dot(skh, qlo, preferred_element_type=F32)
             + jnp.dot(skl_ref[h], qhi, preferred_element_type=F32))
        st_ref[0, h] = s


def _outproj(o2, r, att, x, mod3, w_out, gng, ln_g, ln_b, wqt, skh, skl, alpha, tm):
    B, T, D = x.shape
    W = o2.shape[3]
    tok = lambda w: pl.BlockSpec((1, tm, w), lambda b, t: (b, t, 0))
    modrow = lambda j: pl.BlockSpec((1, 1, D), lambda b, t: (b * N_MOD + j, 0, 0))
    full = lambda a: pl.BlockSpec(a.shape, lambda b, t: (0,) * a.ndim)
    return pl.pallas_call(
        functools.partial(_outproj_kernel, alpha=alpha),
        grid=(B, T // tm),
        in_specs=[pl.BlockSpec((1, 1, tm, W), lambda b, t: (0, b, t, 0)),
                  pl.BlockSpec((1, 1, tm, W), lambda b, t: (1, b, t, 0)),
                  tok(W), tok(att.shape[2]), tok(D), modrow(2), modrow(3), modrow(4),
                  full(w_out), full(gng), full(ln_g), full(ln_b), full(wqt), full(skh), full(skl)],
        out_specs=[tok(D),
                   pl.BlockSpec((1, 1, D, tm), lambda b, t: (b, t, 0, 0)),
                   pl.BlockSpec((1, PEER_HEADS, 2 * N_KEYS, tm), lambda b, t: (b, 0, 0, t))],
        out_shape=[jax.ShapeDtypeStruct((B, T, D), F32),
                   jax.ShapeDtypeStruct((B, T // tm, D, tm), BF16),
                   jax.ShapeDtypeStruct((B, PEER_HEADS, 2 * N_KEYS, T), F32)],
        compiler_params=_params(("parallel", "parallel")),
        name="outproj",
    )(o2, o2, r, att, x, mod3, mod3, mod3, w_out, gng, ln_g, ln_b, wqt, skh, skl)


def _extract_desc(v, count):
    vals = []
    for _ in range(count):
        m = jnp.max(v, axis=0, keepdims=True)
        vals.append(m)
        v = jnp.where(v == m, NEG_INF, v)
    return vals


def _topk_kernel(s_ref, g_ref, s2_ref, kap_ref):
    K = PEER_TOPK
    s1 = s_ref[0, 0, 0:N_KEYS, :]
    s2 = s_ref[0, 0, N_KEYS:2 * N_KEYS, :]
    a = _extract_desc(s1, K + 1)
    b = _extract_desc(s2, K + 1)
    bmat = jnp.concatenate(b[:K], axis=0)
    row8 = lax.broadcasted_iota(jnp.int32, (8, s1.shape[1]), 0)
    cands = [a[0] + bmat]
    for r in range(1, K // 2):
        cands.append(jnp.where(row8 < (K + 1) // (r + 1), a[r] + bmat[0:8], NEG_INF))
    cands.append(jnp.concatenate(a[K // 2:K], axis=0) + b[0])
    cands.append(jnp.where(row8 == 0, a[0] + b[K], jnp.where(row8 == 1, a[K] + b[0], NEG_INF)))
    cand = jnp.concatenate(cands, axis=0)
    top = _extract_desc(cand, K + 1)
    tau = 0.5 * (top[K - 1] + top[K])
    mx = top[0]
    zsum = top[0] * 0.0
    for r in range(K):
        zsum = zsum + jnp.exp(top[r] - mx)
    shift = mx + jnp.log(zsum)
    g = (s1 - shift) * LOG2E
    s2l = s2 * LOG2E
    kap = (tau - shift) * LOG2E
    for ci in range(s1.shape[1] // LANES):
        cols = slice(ci * LANES, (ci + 1) * LANES)
        g_ref[0, 0, ci] = g[:, cols]
        s2_ref[0, 0, ci] = s2l[:, cols]
        kap_ref[0, 0, ci] = kap[:, cols]


def _topk(st, tp):
    B, H, _, T = st.shape
    out = jax.ShapeDtypeStruct((B, H, T // LANES, N_KEYS, LANES), F32)
    ospec = pl.BlockSpec((1, 1, tp // LANES, N_KEYS, LANES), lambda b, h, t: (b, h, t, 0, 0))
    return pl.pallas_call(
        _topk_kernel,
        grid=(B, H, T // tp),
        in_specs=[pl.BlockSpec((1, 1, 2 * N_KEYS, tp), lambda b, h, t: (b, h, 0, t))],
        out_specs=[ospec, ospec,
                   pl.BlockSpec((1, 1, tp // LANES, 1, LANES), lambda b, h, t: (b, h, t, 0, 0))],
        out_shape=[out, out, jax.ShapeDtypeStruct((B, H, T // LANES, 1, LANES), F32)],
        compiler_params=_params(("parallel", "parallel", "parallel")),
        name="topk",
    )(st)


def _peer_kernel(hpt_ref, u_ref, un_ref, vt_ref, vp_ref, s2_ref, g_ref, kap_ref, x1_ref, g2_ref,
                 lng_ref, lnb_ref, o_ref, acc_ref, z0_ref, z1_ref, a0_ref, a1_ref, *, alpha, n_steps):
    e = pl.program_id(2)
    n_chunks = acc_ref.shape[0]
    te = u_ref.shape[0]
    n_i = te // N_KEYS
    sub = PEER_CHUNK // LANES
    upper = (e * n_i) % 8 >= n_i

    @pl.when(e == 0)
    def _():
        acc_ref[...] = jnp.zeros_like(acc_ref)
        a1_ref[...] = jnp.zeros_like(a1_ref)
        z0_ref[...] = jnp.dot(u_ref[...], hpt_ref[0, 0], preferred_element_type=F32)

    def chunk(k, z_cur, a_cur, z_next, a_prev):
        def next_scores():
            if k + 1 < n_chunks:
                z_next[...] = jnp.dot(u_ref[...], hpt_ref[0, k + 1], preferred_element_type=F32)
            else:
                z_next[...] = jnp.dot(un_ref[...], hpt_ref[0, 0], preferred_element_type=F32)

        def prev_output():
            if k > 0:
                acc_ref[k - 1] += jnp.dot(vt_ref[...], a_prev[...], preferred_element_type=F32)
            else:
                acc_ref[n_chunks - 1] += jnp.dot(vp_ref[...], a_prev[...], preferred_element_type=F32)

        def tile(ii, cc):
            rows = slice(ii * N_KEYS, (ii + 1) * N_KEYS)
            cols = slice(cc * LANES, (cc + 1) * LANES)
            c = k * sub + cc
            w = jnp.zeros((N_KEYS, LANES), F32)
            for h in range(PEER_HEADS):
                grow = g_ref[0, h, c, ii:ii + 1, :]
                if 2 * n_i <= 8:
                    grow = jnp.where(upper, g_ref[0, h, c, n_i + ii:n_i + ii + 1, :], grow)
                t = s2_ref[0, h, c] + grow
                w = w + jnp.exp2(jnp.where(t >= kap_ref[0, h, c], t, NEG_INF))
            zt = z_cur[rows, cols]
            act = 0.5 * zt * (1.0 + lax.erf(zt * (2.0 ** -0.5)))
            a_cur[rows, cols] = (w * act).astype(BF16)

        tiles = [(ii, cc) for ii in range(n_i) for cc in range(sub)]
        prev_output()
        for ii, cc in tiles[:len(tiles) // 2]:
            tile(ii, cc)
        next_scores()
        for ii, cc in tiles[len(tiles) // 2:]:
            tile(ii, cc)

    for k in range(n_chunks):
        if k % 2 == 0:
            chunk(k, z0_ref, a0_ref, z1_ref, a1_ref)
        else:
            chunk(k, z1_ref, a1_ref, z0_ref, a0_ref)

    @pl.when(e == n_steps - 1)
    def _():
        acc_ref[n_chunks - 1] += jnp.dot(vt_ref[...], a1_ref[...], preferred_element_type=F32)
        for k in range(n_chunks):
            rows = slice(k * PEER_CHUNK, (k + 1) * PEER_CHUNK)
            y = acc_ref[k].T
            o_ref[0, rows, :] = _layernorm(alpha * x1_ref[0, rows, :] + g2_ref[0] * y,
                                           lng_ref[...], lnb_ref[...])


def _peer(hpt, u, vt, s2, g, kap, x1, mod3, ln_g, ln_b, alpha, tm, te):
    B, _, D, _ = hpt.shape
    T = x1.shape[1]
    E = u.shape[0]
    n_steps = E // te
    nc = tm // LANES
    nk = tm // PEER_CHUNK
    side = pl.BlockSpec((1, PEER_HEADS, nc, N_KEYS, LANES), lambda b, t, e: (b, 0, t, 0, 0))
    n_i = te // N_KEYS
    assert n_i in (4, 8) and nk % 2 == 0
    keyrows = pl.BlockSpec((1, PEER_HEADS, nc, 8, LANES), lambda b, t, e: (b, 0, t, (e * n_i) // 8, 0))
    return pl.pallas_call(
        functools.partial(_peer_kernel, alpha=alpha, n_steps=n_steps),
        grid=(B, T // tm, n_steps),
        in_specs=[pl.BlockSpec((1, nk, D, PEER_CHUNK), lambda b, t, e: (b, t, 0, 0)),
                  pl.BlockSpec((te, D), lambda b, t, e: (e, 0)),
                  pl.BlockSpec((te, D), lambda b, t, e: (jnp.minimum(e + 1, n_steps - 1), 0)),
                  pl.BlockSpec((D, te), lambda b, t, e: (0, e)),
                  pl.BlockSpec((D, te), lambda b, t, e: (0, jnp.maximum(e - 1, 0))),
                  side, keyrows,
                  pl.BlockSpec((1, PEER_HEADS, nc, 1, LANES), lambda b, t, e: (b, 0, t, 0, 0)),
                  pl.BlockSpec((1, tm, D), lambda b, t, e: (b, t, 0)),
                  pl.BlockSpec((1, 1, D), lambda b, t, e: (b * N_MOD + 5, 0, 0)),
                  pl.BlockSpec((1, D), lambda b, t, e: (0, 0)),
                  pl.BlockSpec((1, D), lambda b, t, e: (0, 0))],
        out_specs=pl.BlockSpec((1, tm, D), lambda b, t, e: (b, t, 0)),
        out_shape=jax.ShapeDtypeStruct((B, T, D), F32),
        scratch_shapes=[pltpu.VMEM((nk, D, PEER_CHUNK), F32),
                        pltpu.VMEM((te, PEER_CHUNK), F32), pltpu.VMEM((te, PEER_CHUNK), F32),
                        pltpu.VMEM((te, PEER_CHUNK), BF16), pltpu.VMEM((te, PEER_CHUNK), BF16)],
        compiler_params=_params(("parallel", "parallel", "arbitrary")),
        name="peer",
    )(hpt, u, u, vt, vt, s2, g, kap, x1, mod3, ln_g, ln_b)


def _rope_tables(n_tokens):
    rows = n_tokens // GRID_W
    row = jnp.repeat(jnp.arange(rows, dtype=F32), GRID_W)
    col = jnp.tile(jnp.arange(GRID_W, dtype=F32), rows)
    n_freq = ATT_DH // 4
    inv = ROPE_THETA ** (-jnp.arange(n_freq, dtype=F32) / n_freq)
    ar, ac = row[:, None] * inv, col[:, None] * inv
    cos = jnp.concatenate([jnp.cos(ar), jnp.cos(ar), jnp.cos(ac), jnp.cos(ac)], axis=1)
    sin = jnp.concatenate([-jnp.sin(ar), jnp.sin(ar), -jnp.sin(ac), jnp.sin(ac)], axis=1)
    return jnp.tile(cos, (1, 2)), jnp.tile(sin, (1, 2))


def _dup_heads(w, n_heads, dh):
    return jnp.concatenate([w[:, h * dh:(h + 1) * dh] for h in range(n_heads) for _ in range(2)], axis=1)


def kernel(x, c, ctx, c_ctx, w_ada, b_ada, w_in, w_gate2_f, b_gate_f, w_gate2_b, b_gate_b, gla_norm_g,
           attn_sink, w_out, ln1_g, ln1_b, peer_wq, peer_subkeys, peer_u, peer_v, ln2_g, ln2_b):
    B, T, D = x.shape
    L = ctx.shape[1]
    depth = w_ada.shape[0]
    assert depth == 1, "single-layer block only"
    alpha = (2.0 * depth) ** 0.25

    cond = jnp.concatenate([jax.nn.silu(c), jax.nn.silu(c_ctx)[None]], axis=0)
    cond_bc = jnp.broadcast_to(cond[:, :, None], (B + 1, D, LANES))
    mod = _ada(cond_bc, w_ada[0], b_ada[0][None])
    mod3 = mod.reshape(8 * N_MOD, 1, D)

    wi = w_in[0]
    kq = 2 * GLA_HEADS * GLA_DK + 2 * GLA_HEADS * GLA_DV
    z0, a0 = kq, kq + 2 * GLA_RANK
    k0 = a0 + ATT_HEADS * ATT_DH
    v0 = k0 + ATT_KV_HEADS * ATT_DH
    w_p = jnp.concatenate([
        wi[:, :kq], wi[:, a0:k0],
        _dup_heads(wi[:, k0:v0], ATT_KV_HEADS, ATT_DH),
        _dup_heads(wi[:, v0:], ATT_KV_HEADS, ATT_DH),
        wi[:, z0:a0], jnp.zeros((D, LANES - 2 * GLA_RANK), F32)], axis=1).astype(BF16)
    gw = GLA_HEADS * GLA_DK
    w2 = jnp.zeros((LANES, 2 * gw), F32)
    w2 = w2.at[0:GLA_RANK, 0:gw].set(w_gate2_f[0]).at[GLA_RANK:2 * GLA_RANK, gw:].set(w_gate2_b[0]).astype(BF16)
    bg = jnp.concatenate([b_gate_f[0], b_gate_b[0]])[None]
    cos, sin = _rope_tables(T)
    ones = jnp.ones((L, LANES), F32)

    gq, gk, gv, r, g, aq, ak, av = _inproj(x, mod3, lambda b: b * N_MOD + 1, lambda b: b * N_MOD,
                                           w_p, w2, bg, cos, sin, tm=min(T, 512))
    _, cgk, cgv, _, cg, _, cak, cav = _inproj(ctx, mod3, lambda b: B * N_MOD + 1, lambda b: B * N_MOD,
                                              w_p, w2, bg, ones, 0.0 * ones, tm=L)

    o2 = _gla(gq, gk, g, gv, cgk, cg, cgv, tt=min(T, 512))
    sink_bc = jnp.broadcast_to(attn_sink[0][:, None], (ATT_HEADS, LANES))
    att = _attn(aq, ak, av, cak, cav, sink_bc)

    sk = peer_subkeys[0]
    half = PEER_DQ // 2
    skb = jnp.zeros((PEER_HEADS, 2 * N_KEYS, PEER_DQ), F32)
    skb = skb.at[:, :N_KEYS, :half].set(sk[:, 0]).at[:, N_KEYS:, half:].set(sk[:, 1])
    skh = skb.astype(BF16)
    skl = (skb - skh.astype(F32)).astype(BF16)
    x1, hpt, st = _outproj(o2, r, att, x, mod3, w_out[0].astype(BF16), gla_norm_g[0][None],
                           ln1_g[0][None], ln1_b[0][None], peer_wq[0].T.astype(BF16), skh, skl,
                           alpha, tm=PEER_CHUNK)

    g, s2, kap = _topk(st, tp=min(T, 512))
    u = peer_u[0].astype(BF16)
    vt = peer_v[0].T.astype(BF16)
    return _peer(hpt, u, vt, s2, g, kap, x1, mod3, ln2_g[0][None], ln2_b[0][None],
                 alpha, tm=min(T, 1024), te=512)
```

```python
import functools

import jax
import jax.numpy as jnp
from jax import lax
from jax.experimental import pallas as pl
from jax.experimental.pallas import tpu as pltpu

F32 = jnp.float32
BF16 = jnp.bfloat16

GRID_W = 64
GLA_HEADS = 4
GLA_DK = 64
GLA_DV = 128
GLA_RANK = 16
GLA_NORMALIZER = 16.0
GLA_CHUNK = 64
ATT_HEADS = 8
ATT_KV_HEADS = 2
ATT_DH = 64
WINDOW = 128
ATT_BLOCK = 128
ROPE_THETA = 10000.0
PEER_HEADS = 8
N_KEYS = 128
PEER_DQ = 128
PEER_TOPK = 16
N_MOD = 6
LN_EPS = 1e-5
NEG_INF = -1e30
LANES = 128
PEER_CHUNK = 256
LOG2E = 1.4426950408889634
MXU_TILE = 256

VMEM_LIMIT = 56 * 1024 * 1024

_NT = (((1,), (1,)), ((), ()))
_TN = (((0,), (0,)), ((), ()))


def _params(sem):
    return pltpu.CompilerParams(dimension_semantics=sem, vmem_limit_bytes=VMEM_LIMIT)


def _ada_kernel(s_ref, w_ref, b_ref, o_ref, *, n_rows):
    tn = w_ref.shape[1]
    w = w_ref[...]
    rows = []
    for r in range(n_rows):
        s = s_ref[r]
        cols = [jnp.sum(w[:, j * LANES:(j + 1) * LANES] * s, axis=0, keepdims=True)
                for j in range(tn // LANES)]
        rows.append(jnp.concatenate(cols, axis=1) + b_ref[...])
    rows.append(jnp.zeros((8 - n_rows, tn), F32))
    o_ref[...] = jnp.concatenate(rows, axis=0)


def _ada(cond_bc, w, b):
    n_rows, d, _ = cond_bc.shape
    n = w.shape[1]
    tn = 512
    return pl.pallas_call(
        functools.partial(_ada_kernel, n_rows=n_rows),
        grid=(n // tn,),
        in_specs=[pl.BlockSpec((n_rows, d, LANES), lambda j: (0, 0, 0)),
                  pl.BlockSpec((d, tn), lambda j: (0, j)),
                  pl.BlockSpec((1, tn), lambda j: (0, j))],
        out_specs=pl.BlockSpec((8, tn), lambda j: (0, j)),
        out_shape=jax.ShapeDtypeStruct((8, n), F32),
        compiler_params=_params(("parallel",)),
        name="ada",
    )(cond_bc, w, b)


def _rope(a, cos, sin):
    n = a.shape[1] // LANES
    c = jnp.concatenate([cos] * n, axis=1)
    s = jnp.concatenate([sin] * n, axis=1)
    lane = lax.broadcasted_iota(jnp.int32, a.shape, 1)
    swapped = jnp.where((lane % 32) < 16,
                        pltpu.roll(a, a.shape[1] - 16, 1),
                        pltpu.roll(a, 16, 1))
    return a * c + swapped * s


def _inproj_kernel(x_ref, sc_ref, sh_ref, w_ref, w2_ref, bg_ref, cos_ref, sin_ref,
                   gq_ref, gk_ref, gv_ref, r_ref, g_ref, aq_ref, ak_ref, av_ref):
    h = x_ref[0] * (1.0 + sc_ref[0]) + sh_ref[0]
    y = jnp.dot(h.astype(BF16), w_ref[...], preferred_element_type=F32)
    gq_ref[0] = y[:, 0:256] * (GLA_DK ** -0.5)
    gk_ref[0] = y[:, 256:512]
    gv_ref[0] = y[:, 512:1024].astype(BF16)
    r_ref[0] = y[:, 1024:1536]
    cos = cos_ref[...]
    sin = sin_ref[...]
    aq_ref[0] = (_rope(y[:, 1536:2048], cos, sin) * (ATT_DH ** -0.5)).astype(BF16)
    ak_ref[0] = _rope(y[:, 2048:2304], cos, sin).astype(BF16)
    av_ref[0] = y[:, 2304:2560].astype(BF16)
    z = y[:, 2560:2688]
    logits = jnp.dot(z.astype(BF16), w2_ref[...], preferred_element_type=F32) + bg_ref[...]
    g = (jnp.minimum(logits, 0.0) - jnp.log1p(jnp.exp(-jnp.abs(logits)))) * (1.0 / GLA_NORMALIZER)
    g_ref[0, 0] = g[:, 0:256]
    g_ref[1, 0] = g[:, 256:512]


def _inproj(x, mod3, sc_row, sh_row, w_p, w2, bg, cos, sin, tm):
    B, T, D = x.shape
    NP = w_p.shape[1]
    tok = lambda w, dt: jax.ShapeDtypeStruct((B, T, w), dt)
    tspec = lambda w: pl.BlockSpec((1, tm, w), lambda b, t: (b, t, 0))
    return pl.pallas_call(
        _inproj_kernel,
        grid=(B, T // tm),
        in_specs=[tspec(D),
                  pl.BlockSpec((1, 1, D), lambda b, t: (sc_row(b), 0, 0)),
                  pl.BlockSpec((1, 1, D), lambda b, t: (sh_row(b), 0, 0)),
                  pl.BlockSpec((D, NP), lambda b, t: (0, 0)),
                  pl.BlockSpec((LANES, 512), lambda b, t: (0, 0)),
                  pl.BlockSpec((1, 512), lambda b, t: (0, 0)),
                  pl.BlockSpec((tm, LANES), lambda b, t: (t, 0)),
                  pl.BlockSpec((tm, LANES), lambda b, t: (t, 0))],
        out_specs=[tspec(256), tspec(256), tspec(512), tspec(512),
                   pl.BlockSpec((2, 1, tm, 256), lambda b, t: (0, b, t, 0)),
                   tspec(512), tspec(256), tspec(256)],
        out_shape=[tok(256, F32), tok(256, F32), tok(512, BF16), tok(512, F32),
                   jax.ShapeDtypeStruct((2, B, T, 256), F32),
                   tok(512, BF16), tok(256, BF16), tok(256, BF16)],
        compiler_params=_params(("parallel", "parallel")),
        name="inproj",
    )(x, mod3, mod3, w_p, w2, bg, cos, sin)


def _gla_kernel(qf_ref, kf_ref, gf_ref, vf_ref, qb_ref, kb_ref, gb_ref, vb_ref, ck_ref, cg_ref, cv_ref,
                of_ref, ob_ref, st_ref, *, n_chunks, n_ctx_chunks):
    t = pl.program_id(1)
    C = GLA_CHUNK
    npairs = GLA_HEADS // 2
    ri = lax.broadcasted_iota(jnp.int32, (C, C), 0)
    ci = lax.broadcasted_iota(jnp.int32, (C, C), 1)
    tri = [ri >= ci, ri <= ci]
    tri_bf = [m.astype(BF16) for m in tri]
    tri2 = [jnp.concatenate([m, m], axis=0) for m in tri]
    lane = lax.broadcasted_iota(jnp.int32, (1, LANES), 1)
    m0 = (lane < GLA_DK).astype(F32)
    m1 = 1.0 - m0
    rr = lax.broadcasted_iota(jnp.int32, (2 * GLA_DV, LANES), 0)
    ll = lax.broadcasted_iota(jnp.int32, (2 * GLA_DV, LANES), 1)
    blockdiag = (rr < GLA_DV) == (ll < GLA_DK)

    def chunks(items, with_out):
        bs = []
        for d, _, _, gc, _, _ in items:
            ghi = gc.astype(BF16)
            glo = (gc - ghi.astype(F32)).astype(BF16)
            bs.append(jnp.dot(tri_bf[d], ghi, preferred_element_type=F32)
                      + jnp.dot(tri_bf[d], glo, preferred_element_type=F32))
        btots = [jnp.sum(it[3], axis=0, keepdims=True) for it in items]
        sts = [st_ref[it[0], it[1]] for it in items]
        outs = [None] * len(items)
        if with_out:
            qes = [it[5] * jnp.exp(b) for it, b in zip(items, bs)]
            amats = []
            for it, b, qe in zip(items, bs, qes):
                ke = (it[2] * jnp.exp(-b)).astype(BF16)
                lhs = jnp.concatenate([qe * m0, qe * m1], axis=0).astype(BF16)
                amats.append(lax.dot_general(lhs, ke, _NT, preferred_element_type=F32))
            inters = [lax.dot_general(qe.astype(BF16), st.astype(BF16), _NT, preferred_element_type=F32)
                      for qe, st in zip(qes, sts)]
            for n, (it, a) in enumerate(zip(items, amats)):
                a = jnp.where(tri2[it[0]], a, 0.0).astype(BF16)
                oi = jnp.dot(a, it[4], preferred_element_type=F32)
                outs[n] = jnp.concatenate([oi[:C, :GLA_DV], oi[C:, GLA_DV:]], axis=1) + inters[n]
        for it, b, btot, st in zip(items, bs, btots, sts):
            kd = (it[2] * jnp.exp(btot - b)).astype(BF16)
            mt = lax.dot_general(it[4], kd, _TN, preferred_element_type=F32)
            st_ref[it[0], it[1]] = st * jnp.exp(btot) + jnp.where(blockdiag, mt, 0.0)
        return outs

    def qk_lanes(p):
        return slice(p * LANES, (p + 1) * LANES)

    def v_lanes(p):
        return slice(p * 2 * GLA_DV, (p + 1) * 2 * GLA_DV)

    @pl.when(t == 0)
    def _():
        st_ref[...] = jnp.zeros_like(st_ref)
        for j in range(n_ctx_chunks):
            items = []
            for d in range(2):
                cj = j if d == 0 else n_ctx_chunks - 1 - j
                rows = slice(cj * C, (cj + 1) * C)
                items += [(d, p, ck_ref[0, rows, qk_lanes(p)], cg_ref[d, 0, rows, qk_lanes(p)],
                           cv_ref[0, rows, v_lanes(p)], None) for p in range(npairs)]
            chunks(items, False)

    streams = [(qf_ref, kf_ref, gf_ref, vf_ref, of_ref), (qb_ref, kb_ref, gb_ref, vb_ref, ob_ref)]

    def body(j, carry):
        items, dests = [], []
        for d, (q_ref, k_ref, g_ref, v_ref, o_ref) in enumerate(streams):
            cj = j if d == 0 else n_chunks - 1 - j
            rows = pl.ds(pl.multiple_of(cj * C, C), C)
            for p in range(npairs):
                items.append((d, p, k_ref[0, rows, qk_lanes(p)], g_ref[0, 0, rows, qk_lanes(p)],
                              v_ref[0, rows, v_lanes(p)], q_ref[0, rows, qk_lanes(p)]))
                dests.append((o_ref, rows, v_lanes(p)))
        for (o_ref, rows, lanes), out in zip(dests, chunks(items, True)):
            o_ref[0, rows, lanes] = out
        return carry

    lax.fori_loop(0, n_chunks, body, 0)


def _gla(gq, gk, g, gv, ck, cg, cv, tt):
    B, T, W = gq.shape
    L = ck.shape[1]
    nT = T // tt
    VW = gv.shape[2]
    fwd = lambda b, t: (b, t, 0)
    bwd = lambda b, t: (b, nT - 1 - t, 0)
    ctx = lambda b, t: (b, 0, 0)
    out = jax.ShapeDtypeStruct((B, T, VW), F32)
    return pl.pallas_call(
        functools.partial(_gla_kernel, n_chunks=tt // GLA_CHUNK, n_ctx_chunks=L // GLA_CHUNK),
        grid=(B, nT),
        in_specs=[pl.BlockSpec((1, tt, W), fwd), pl.BlockSpec((1, tt, W), fwd),
                  pl.BlockSpec((1, 1, tt, W), lambda b, t: (0, b, t, 0)),
                  pl.BlockSpec((1, tt, VW), fwd),
                  pl.BlockSpec((1, tt, W), bwd), pl.BlockSpec((1, tt, W), bwd),
                  pl.BlockSpec((1, 1, tt, W), lambda b, t: (1, b, nT - 1 - t, 0)),
                  pl.BlockSpec((1, tt, VW), bwd),
                  pl.BlockSpec((1, L, W), ctx),
                  pl.BlockSpec((2, 1, L, W), lambda b, t: (0, b, 0, 0)),
                  pl.BlockSpec((1, L, VW), ctx)],
        out_specs=[pl.BlockSpec((1, tt, VW), fwd), pl.BlockSpec((1, tt, VW), bwd)],
        out_shape=[out, out],
        scratch_shapes=[pltpu.VMEM((2, GLA_HEADS // 2, 2 * GLA_DV, LANES), F32)],
        compiler_params=_params(("parallel", "arbitrary")),
        name="gla",
    )(gq, gk, g, gv, gq, gk, g, gv, ck, cg, cv)


def _attn_kernel(q_ref, kp_ref, kc_ref, kn_ref, vp_ref, vc_ref, vn_ref, ck_ref, cv_ref, sink_ref,
                 o_ref, *, n_blocks):
    n = pl.program_id(1)
    BLK = ATT_BLOCK
    L = ck_ref.shape[1]
    nk = 3 * BLK + L
    group = ATT_HEADS // ATT_KV_HEADS
    ki = lax.broadcasted_iota(jnp.int32, (nk, group * BLK), 0)
    qi = lax.broadcasted_iota(jnp.int32, (nk, group * BLK), 1) & (BLK - 1)
    band = jnp.abs(ki - BLK - qi) <= WINDOW
    prev_ok = jnp.logical_or(ki >= BLK, n > 0)
    next_ok = jnp.logical_or(ki < 2 * BLK, n < n_blocks - 1)
    valid = jnp.logical_or(ki >= 3 * BLK, band & prev_ok & next_ok)
    lane = lax.broadcasted_iota(jnp.int32, (BLK, LANES), 1)
    zero = jnp.zeros((), BF16)
    for kv in range(ATT_KV_HEADS):
        ks = slice(kv * LANES, (kv + 1) * LANES)
        kcat = jnp.concatenate([kp_ref[0, :, ks], kc_ref[0, :, ks], kn_ref[0, :, ks], ck_ref[0, :, ks]], axis=0)
        vcat = jnp.concatenate([vp_ref[0, :, ks], vc_ref[0, :, ks], vn_ref[0, :, ks], cv_ref[0, :, ks]], axis=0)
        qparts = []
        for gi in range(group // 2):
            q2 = q_ref[0, :, (kv * group // 2 + gi) * LANES:(kv * group // 2 + gi + 1) * LANES]
            qparts += [jnp.where(lane < ATT_DH, q2, zero), jnp.where(lane >= ATT_DH, q2, zero)]
        qstack = jnp.concatenate(qparts, axis=0)
        s = lax.dot_general(kcat, qstack, _NT, preferred_element_type=F32)
        s = jnp.where(valid, s, NEG_INF)
        snk = jnp.concatenate([sink_ref[kv * group + hh:kv * group + hh + 1, :] for hh in range(group)], axis=1)
        m = jnp.maximum(jnp.max(s, axis=0, keepdims=True), snk)
        p = jnp.exp(s - m)
        den = jnp.sum(p, axis=0, keepdims=True) + jnp.exp(snk - m)
        ot = lax.dot_general(vcat, p.astype(BF16), _TN, preferred_element_type=F32)
        ot = ot[0:ATT_DH] / den
        for gi in range(group // 2):
            pair = jnp.concatenate([ot[:, (2 * gi) * BLK:(2 * gi + 1) * BLK],
                                    ot[:, (2 * gi + 1) * BLK:(2 * gi + 2) * BLK]], axis=0)
            cols = slice((kv * group // 2 + gi) * LANES, (kv * group // 2 + gi + 1) * LANES)
            o_ref[0, :, cols] = pair.T.astype(o_ref.dtype)


def _attn(aq, ak, av, ck, cv, sink_bc):
    B, T, _ = aq.shape
    L = ck.shape[1]
    nb = T // ATT_BLOCK
    kvw = ak.shape[2]
    cur = lambda b, n: (b, n, 0)
    prev = lambda b, n: (b, jnp.maximum(n - 1, 0), 0)
    nxt = lambda b, n: (b, jnp.minimum(n + 1, nb - 1), 0)
    kspec = lambda im: pl.BlockSpec((1, ATT_BLOCK, kvw), im)
    return pl.pallas_call(
        functools.partial(_attn_kernel, n_blocks=nb),
        grid=(B, nb),
        in_specs=[pl.BlockSpec((1, ATT_BLOCK, ATT_HEADS * ATT_DH), cur),
                  kspec(prev), kspec(cur), kspec(nxt), kspec(prev), kspec(cur), kspec(nxt),
                  pl.BlockSpec((1, L, kvw), lambda b, n: (b, 0, 0)),
                  pl.BlockSpec((1, L, kvw), lambda b, n: (b, 0, 0)),
                  pl.BlockSpec((ATT_HEADS, LANES), lambda b, n: (0, 0))],
        out_specs=pl.BlockSpec((1, ATT_BLOCK, ATT_HEADS * ATT_DH), cur),
        out_shape=jax.ShapeDtypeStruct((B, T, ATT_HEADS * ATT_DH), BF16),
        compiler_params=_params(("parallel", "parallel")),
        name="attn",
    )(aq, ak, ak, ak, av, av, av, ck, cv, sink_bc)


def _layernorm(v, g, b):
    mu = jnp.mean(v, axis=-1, keepdims=True)
    vc = v - mu
    var = jnp.mean(vc * vc, axis=-1, keepdims=True)
    return vc * lax.rsqrt(var + LN_EPS) * g + b


def _outproj_kernel(of_ref, ob_ref, r_ref, att_ref, x_ref, g1_ref, sh2_ref, sc2_ref,
                    wout_ref, gng_ref, lng_ref, lnb_ref, wqt_ref, skh_ref, skl_ref,
                    x1_ref, hpt_ref, st_ref, *, alpha):
    o = of_ref[0] + ob_ref[0]
    r = r_ref[0]
    gng = gng_ref[...]
    parts = []
    for h in range(GLA_HEADS):
        oh = o[:, h * GLA_DV:(h + 1) * GLA_DV]
        oh = oh * lax.rsqrt(jnp.mean(oh * oh, axis=-1, keepdims=True) + LN_EPS) * gng
        rh = r[:, h * GLA_DV:(h + 1) * GLA_DV]
        parts.append((oh * (rh * jax.nn.sigmoid(rh))).astype(BF16))
    parts.append(att_ref[0])
    cat = jnp.concatenate(parts, axis=1)
    y = jnp.dot(cat, wout_ref[...], preferred_element_type=F32)
    x1 = _layernorm(alpha * x_ref[0] + g1_ref[0] * y, lng_ref[...], lnb_ref[...])
    x1_ref[0] = x1
    hp = x1 * (1.0 + sc2_ref[0]) + sh2_ref[0]
    hpt = hp.T.astype(BF16)
    hpt_ref[0, 0] = hpt
    qt = jnp.dot(wqt_ref[...], hpt, preferred_element_type=F32)
    for h in range(PEER_HEADS):
        qh = qt[h * PEER_DQ:(h + 1) * PEER_DQ, :]
        qhi = qh.astype(BF16)
        qlo = (qh - qhi.astype(F32)).astype(BF16)
        skh = skh_ref[h]
        s = (jnp.dot(skh, qhi, preferred_element_type=F32)
             + jnp.dot(skh, qlo, preferred_element_type=F32)
             + jnp.dot(skl_ref[h], qhi, preferred_element_type=F32))
        st_ref[0, h] = s


def _outproj(of, ob, r, att, x, mod3, w_out, gng, ln_g, ln_b, wqt, skh, skl, alpha, tm):
    B, T, D = x.shape
    W = of.shape[2]
    tok = lambda w: pl.BlockSpec((1, tm, w), lambda b, t: (b, t, 0))
    modrow = lambda j: pl.BlockSpec((1, 1, D), lambda b, t: (b * N_MOD + j, 0, 0))
    full = lambda a: pl.BlockSpec(a.shape, lambda b, t: (0,) * a.ndim)
    return pl.pallas_call(
        functools.partial(_outproj_kernel, alpha=alpha),
        grid=(B, T // tm),
        in_specs=[tok(W), tok(W), tok(W), tok(att.shape[2]), tok(D), modrow(2), modrow(3), modrow(4),
                  full(w_out), full(gng), full(ln_g), full(ln_b), full(wqt), full(skh), full(skl)],
        out_specs=[tok(D),
                   pl.BlockSpec((1, 1, D, tm), lambda b, t: (b, t, 0, 0)),
                   pl.BlockSpec((1, PEER_HEADS, 2 * N_KEYS, tm), lambda b, t: (b, 0, 0, t))],
        out_shape=[jax.ShapeDtypeStruct((B, T, D), F32),
                   jax.ShapeDtypeStruct((B, T // tm, D, tm), BF16),
                   jax.ShapeDtypeStruct((B, PEER_HEADS, 2 * N_KEYS, T), F32)],
        compiler_params=_params(("parallel", "parallel")),
        name="outproj",
    )(of, ob, r, att, x, mod3, mod3, mod3, w_out, gng, ln_g, ln_b, wqt, skh, skl)


def _extract_desc(v, count):
    vals = []
    for _ in range(count):
        m = jnp.max(v, axis=0, keepdims=True)
        vals.append(m)
        v = jnp.where(v == m, NEG_INF, v)
    return vals


def _topk_kernel(s_ref, g_ref, s2_ref, kap_ref):
    K = PEER_TOPK
    s1 = s_ref[0, 0, 0:N_KEYS, :]
    s2 = s_ref[0, 0, N_KEYS:2 * N_KEYS, :]
    a = _extract_desc(s1, K + 1)
    b = _extract_desc(s2, K + 1)
    bmat = jnp.concatenate(b[:K], axis=0)
    row8 = lax.broadcasted_iota(jnp.int32, (8, s1.shape[1]), 0)
    cands = [a[0] + bmat]
    for r in range(1, K // 2):
        cands.append(jnp.where(row8 < (K + 1) // (r + 1), a[r] + bmat[0:8], NEG_INF))
    cands.append(jnp.concatenate(a[K // 2:K], axis=0) + b[0])
    cands.append(jnp.where(row8 == 0, a[0] + b[K], jnp.where(row8 == 1, a[K] + b[0], NEG_INF)))
    cand = jnp.concatenate(cands, axis=0)
    top = _extract_desc(cand, K + 1)
    tau = 0.5 * (top[K - 1] + top[K])
    mx = top[0]
    zsum = top[0] * 0.0
    for r in range(K):
        zsum = zsum + jnp.exp(top[r] - mx)
    shift = mx + jnp.log(zsum)
    g = (s1 - shift) * LOG2E
    s2l = s2 * LOG2E
    kap = (tau - shift) * LOG2E
    for ci in range(s1.shape[1] // LANES):
        cols = slice(ci * LANES, (ci + 1) * LANES)
        g_ref[0, 0, ci] = g[:, cols]
        s2_ref[0, 0, ci] = s2l[:, cols]
        kap_ref[0, 0, ci] = kap[:, cols]


def _topk(st, tp):
    B, H, _, T = st.shape
    out = jax.ShapeDtypeStruct((B, H, T // LANES, N_KEYS, LANES), F32)
    ospec = pl.BlockSpec((1, 1, tp // LANES, N_KEYS, LANES), lambda b, h, t: (b, h, t, 0, 0))
    return pl.pallas_call(
        _topk_kernel,
        grid=(B, H, T // tp),
        in_specs=[pl.BlockSpec((1, 1, 2 * N_KEYS, tp), lambda b, h, t: (b, h, 0, t))],
        out_specs=[ospec, ospec,
                   pl.BlockSpec((1, 1, tp // LANES, 1, LANES), lambda b, h, t: (b, h, t, 0, 0))],
        out_shape=[out, out, jax.ShapeDtypeStruct((B, H, T // LANES, 1, LANES), F32)],
        compiler_params=_params(("parallel", "parallel", "parallel")),
        name="topk",
    )(st)


def _peer_kernel(hpt_ref, u_ref, un_ref, vt_ref, vp_ref, s2_ref, g_ref, kap_ref, x1_ref, g2_ref,
                 lng_ref, lnb_ref, o_ref, acc_ref, z0_ref, z1_ref, a0_ref, a1_ref, *, alpha, n_steps):
    e = pl.program_id(2)
    n_chunks = acc_ref.shape[0]
    te = u_ref.shape[0]
    n_i = te // N_KEYS
    sub = PEER_CHUNK // LANES
    upper = (e * n_i) % 8 >= n_i

    @pl.when(e == 0)
    def _():
        acc_ref[...] = jnp.zeros_like(acc_ref)
        a1_ref[...] = jnp.zeros_like(a1_ref)
        z0_ref[...] = jnp.dot(u_ref[...], hpt_ref[0, 0], preferred_element_type=F32)

    def chunk(k, z_cur, a_cur, z_next, a_prev):
        def next_scores():
            if k + 1 < n_chunks:
                z_next[...] = jnp.dot(u_ref[...], hpt_ref[0, k + 1], preferred_element_type=F32)
            else:
                z_next[...] = jnp.dot(un_ref[...], hpt_ref[0, 0], preferred_element_type=F32)

        def prev_output():
            if k > 0:
                acc_ref[k - 1] += jnp.dot(vt_ref[...], a_prev[...], preferred_element_type=F32)
            else:
                acc_ref[n_chunks - 1] += jnp.dot(vp_ref[...], a_prev[...], preferred_element_type=F32)

        def tile(ii, cc):
            rows = slice(ii * N_KEYS, (ii + 1) * N_KEYS)
            cols = slice(cc * LANES, (cc + 1) * LANES)
            c = k * sub + cc
            w = jnp.zeros((N_KEYS, LANES), F32)
            for h in range(PEER_HEADS):
                grow = g_ref[0, h, c, ii:ii + 1, :]
                if 2 * n_i <= 8:
                    grow = jnp.where(upper, g_ref[0, h, c, n_i + ii:n_i + ii + 1, :], grow)
                t = s2_ref[0, h, c] + grow
                w = w + jnp.exp2(jnp.where(t >= kap_ref[0, h, c], t, NEG_INF))
            zt = z_cur[rows, cols]
            act = 0.5 * zt * (1.0 + lax.erf(zt * (2.0 ** -0.5)))
            a_cur[rows, cols] = (w * act).astype(BF16)

        tiles = [(ii, cc) for ii in range(n_i) for cc in range(sub)]
        prev_output()
        for ii, cc in tiles[:len(tiles) // 2]:
            tile(ii, cc)
        next_scores()
        for ii, cc in tiles[len(tiles) // 2:]:
            tile(ii, cc)

    for k in range(n_chunks):
        if k % 2 == 0:
            chunk(k, z0_ref, a0_ref, z1_ref, a1_ref)
        else:
            chunk(k, z1_ref, a1_ref, z0_ref, a0_ref)

    @pl.when(e == n_steps - 1)
    def _():
        acc_ref[n_chunks - 1] += jnp.dot(vt_ref[...], a1_ref[...], preferred_element_type=F32)
        for k in range(n_chunks):
            rows = slice(k * PEER_CHUNK, (k + 1) * PEER_CHUNK)
            y = acc_ref[k].T
            o_ref[0, rows, :] = _layernorm(alpha * x1_ref[0, rows, :] + g2_ref[0] * y,
                                           lng_ref[...], lnb_ref[...])


def _peer(hpt, u, vt, s2, g, kap, x1, mod3, ln_g, ln_b, alpha, tm, te):
    B, _, D, _ = hpt.shape
    T = x1.shape[1]
    E = u.shape[0]
    n_steps = E // te
    nc = tm // LANES
    nk = tm // PEER_CHUNK
    side = pl.BlockSpec((1, PEER_HEADS, nc, N_KEYS, LANES), lambda b, t, e: (b, 0, t, 0, 0))
    n_i = te // N_KEYS
    assert n_i in (4, 8) and nk % 2 == 0
    keyrows = pl.BlockSpec((1, PEER_HEADS, nc, 8, LANES), lambda b, t, e: (b, 0, t, (e * n_i) // 8, 0))
    return pl.pallas_call(
        functools.partial(_peer_kernel, alpha=alpha, n_steps=n_steps),
        grid=(B, T // tm, n_steps),
        in_specs=[pl.BlockSpec((1, nk, D, PEER_CHUNK), lambda b, t, e: (b, t, 0, 0)),
                  pl.BlockSpec((te, D), lambda b, t, e: (e, 0)),
                  pl.BlockSpec((te, D), lambda b, t, e: (jnp.minimum(e + 1, n_steps - 1), 0)),
                  pl.BlockSpec((D, te), lambda b, t, e: (0, e)),
                  pl.BlockSpec((D, te), lambda b, t, e: (0, jnp.maximum(e - 1, 0))),
                  side, keyrows,
                  pl.BlockSpec((1, PEER_HEADS, nc, 1, LANES), lambda b, t, e: (b, 0, t, 0, 0)),
                  pl.BlockSpec((1, tm, D), lambda b, t, e: (b, t, 0)),
                  pl.BlockSpec((1, 1, D), lambda b, t, e: (b * N_MOD + 5, 0, 0)),
                  pl.BlockSpec((1, D), lambda b, t, e: (0, 0)),
                  pl.BlockSpec((1, D), lambda b, t, e: (0, 0))],
        out_specs=pl.BlockSpec((1, tm, D), lambda b, t, e: (b, t, 0)),
        out_shape=jax.ShapeDtypeStruct((B, T, D), F32),
        scratch_shapes=[pltpu.VMEM((nk, D, PEER_CHUNK), F32),
                        pltpu.VMEM((te, PEER_CHUNK), F32), pltpu.VMEM((te, PEER_CHUNK), F32),
                        pltpu.VMEM((te, PEER_CHUNK), BF16), pltpu.VMEM((te, PEER_CHUNK), BF16)],
        compiler_params=_params(("parallel", "parallel", "arbitrary")),
        name="peer",
    )(hpt, u, u, vt, vt, s2, g, kap, x1, mod3, ln_g, ln_b)


def _rope_tables(n_tokens):
    rows = n_tokens // GRID_W
    row = jnp.repeat(jnp.arange(rows, dtype=F32), GRID_W)
    col = jnp.tile(jnp.arange(GRID_W, dtype=F32), rows)
    n_freq = ATT_DH // 4
    inv = ROPE_THETA ** (-jnp.arange(n_freq, dtype=F32) / n_freq)
    ar, ac = row[:, None] * inv, col[:, None] * inv
    cos = jnp.concatenate([jnp.cos(ar), jnp.cos(ar), jnp.cos(ac), jnp.cos(ac)], axis=1)
    sin = jnp.concatenate([-jnp.sin(ar), jnp.sin(ar), -jnp.sin(ac), jnp.sin(ac)], axis=1)
    return jnp.tile(cos, (1, 2)), jnp.tile(sin, (1, 2))


def _dup_heads(w, n_heads, dh):
    return jnp.concatenate([w[:, h * dh:(h + 1) * dh] for h in range(n_heads) for _ in range(2)], axis=1)


def kernel(x, c, ctx, c_ctx, w_ada, b_ada, w_in, w_gate2_f, b_gate_f, w_gate2_b, b_gate_b, gla_norm_g,
           attn_sink, w_out, ln1_g, ln1_b, peer_wq, peer_subkeys, peer_u, peer_v, ln2_g, ln2_b):
    B, T, D = x.shape
    L = ctx.shape[1]
    depth = w_ada.shape[0]
    assert depth == 1, "single-layer block only"
    alpha = (2.0 * depth) ** 0.25

    cond = jnp.concatenate([jax.nn.silu(c), jax.nn.silu(c_ctx)[None]], axis=0)
    cond_bc = jnp.broadcast_to(cond[:, :, None], (B + 1, D, LANES))
    mod = _ada(cond_bc, w_ada[0], b_ada[0][None])
    mod3 = mod.reshape(8 * N_MOD, 1, D)

    wi = w_in[0]
    kq = 2 * GLA_HEADS * GLA_DK + 2 * GLA_HEADS * GLA_DV
    z0, a0 = kq, kq + 2 * GLA_RANK
    k0 = a0 + ATT_HEADS * ATT_DH
    v0 = k0 + ATT_KV_HEADS * ATT_DH
    w_p = jnp.concatenate([
        wi[:, :kq], wi[:, a0:k0],
        _dup_heads(wi[:, k0:v0], ATT_KV_HEADS, ATT_DH),
        _dup_heads(wi[:, v0:], ATT_KV_HEADS, ATT_DH),
        wi[:, z0:a0], jnp.zeros((D, LANES - 2 * GLA_RANK), F32)], axis=1).astype(BF16)
    gw = GLA_HEADS * GLA_DK
    w2 = jnp.zeros((LANES, 2 * gw), F32)
    w2 = w2.at[0:GLA_RANK, 0:gw].set(w_gate2_f[0]).at[GLA_RANK:2 * GLA_RANK, gw:].set(w_gate2_b[0]).astype(BF16)
    bg = jnp.concatenate([b_gate_f[0], b_gate_b[0]])[None]
    cos, sin = _rope_tables(T)
    ones = jnp.ones((L, LANES), F32)

    gq, gk, gv, r, g, aq, ak, av = _inproj(x, mod3, lambda b: b * N_MOD + 1, lambda b: b * N_MOD,
                                           w_p, w2, bg, cos, sin, tm=min(T, 512))
    _, cgk, cgv, _, cg, _, cak, cav = _inproj(ctx, mod3, lambda b: B * N_MOD + 1, lambda b: B * N_MOD,
                                              w_p, w2, bg, ones, 0.0 * ones, tm=L)

    of, ob = _gla(gq, gk, g, gv, cgk, cg, cgv, tt=min(T, 512))
    sink_bc = jnp.broadcast_to(attn_sink[0][:, None], (ATT_HEADS, LANES))
    att = _attn(aq, ak, av, cak, cav, sink_bc)

    sk = peer_subkeys[0]
    half = PEER_DQ // 2
    skb = jnp.zeros((PEER_HEADS, 2 * N_KEYS, PEER_DQ), F32)
    skb = skb.at[:, :N_KEYS, :half].set(sk[:, 0]).at[:, N_KEYS:, half:].set(sk[:, 1])
    skh = skb.astype(BF16)
    skl = (skb - skh.astype(F32)).astype(BF16)
    x1, hpt, st = _outproj(of, ob, r, att, x, mod3, w_out[0].astype(BF16), gla_norm_g[0][None],
                           ln1_g[0][None], ln1_b[0][None], peer_wq[0].T.astype(BF16), skh, skl,
                           alpha, tm=PEER_CHUNK)

    g, s2, kap = _topk(st, tp=min(T, 512))
    u = peer_u[0].astype(BF16)
    vt = peer_v[0].T.astype(BF16)
    return _peer(hpt, u, vt, s2, g, kap, x1, mod3, ln2_g[0][None], ln2_b[0][None],
                 alpha, tm=min(T, 1024), te=512)
```

```python
import functools

import jax
import jax.numpy as jnp
from jax import lax
from jax.experimental import pallas as pl
from jax.experimental.pallas import tpu as pltpu

F32 = jnp.float32
BF16 = jnp.bfloat16

GRID_W = 64
GLA_HEADS = 4
GLA_DK = 64
GLA_DV = 128
GLA_RANK = 16
GLA_NORMALIZER = 16.0
GLA_CHUNK = 64
ATT_HEADS = 8
ATT_KV_HEADS = 2
ATT_DH = 64
WINDOW = 128
ATT_BLOCK = 128
ROPE_THETA = 10000.0
PEER_HEADS = 8
N_KEYS = 128
PEER_DQ = 128
PEER_TOPK = 16
N_MOD = 6
LN_EPS = 1e-5
NEG_INF = -1e30
LANES = 128
PEER_CHUNK = 256
LOG2E = 1.4426950408889634
MXU_TILE = 256

VMEM_LIMIT = 56 * 1024 * 1024

_NT = (((1,), (1,)), ((), ()))
_TN = (((0,), (0,)), ((), ()))


def _params(sem):
    return pltpu.CompilerParams(dimension_semantics=sem, vmem_limit_bytes=VMEM_LIMIT)


def _ada_kernel(s_ref, w_ref, b_ref, o_ref, *, n_rows):
    tn = w_ref.shape[1]
    w = w_ref[...]
    rows = []
    for r in range(n_rows):
        s = s_ref[r]
        cols = [jnp.sum(w[:, j * LANES:(j + 1) * LANES] * s, axis=0, keepdims=True)
                for j in range(tn // LANES)]
        rows.append(jnp.concatenate(cols, axis=1) + b_ref[...])
    rows.append(jnp.zeros((8 - n_rows, tn), F32))
    o_ref[...] = jnp.concatenate(rows, axis=0)


def _ada(cond_bc, w, b):
    n_rows, d, _ = cond_bc.shape
    n = w.shape[1]
    tn = 512
    return pl.pallas_call(
        functools.partial(_ada_kernel, n_rows=n_rows),
        grid=(n // tn,),
        in_specs=[pl.BlockSpec((n_rows, d, LANES), lambda j: (0, 0, 0)),
                  pl.BlockSpec((d, tn), lambda j: (0, j)),
                  pl.BlockSpec((1, tn), lambda j: (0, j))],
        out_specs=pl.BlockSpec((8, tn), lambda j: (0, j)),
        out_shape=jax.ShapeDtypeStruct((8, n), F32),
        compiler_params=_params(("parallel",)),
        name="ada",
    )(cond_bc, w, b)


def _rope(a, cos, sin):
    n = a.shape[1] // LANES
    c = jnp.concatenate([cos] * n, axis=1)
    s = jnp.concatenate([sin] * n, axis=1)
    lane = lax.broadcasted_iota(jnp.int32, a.shape, 1)
    swapped = jnp.where((lane % 32) < 16,
                        pltpu.roll(a, a.shape[1] - 16, 1),
                        pltpu.roll(a, 16, 1))
    return a * c + swapped * s


def _inproj_kernel(x_ref, sc_ref, sh_ref, w_ref, w2_ref, bg_ref, cos_ref, sin_ref,
                   gq_ref, gk_ref, gv_ref, r_ref, g_ref, aq_ref, ak_ref, av_ref):
    h = x_ref[0] * (1.0 + sc_ref[0]) + sh_ref[0]
    y = jnp.dot(h.astype(BF16), w_ref[...], preferred_element_type=F32)
    gq_ref[0] = y[:, 0:256] * (GLA_DK ** -0.5)
    gk_ref[0] = y[:, 256:512]
    gv_ref[0] = y[:, 512:1024].astype(BF16)
    r_ref[0] = y[:, 1024:1536]
    cos = cos_ref[...]
    sin = sin_ref[...]
    aq_ref[0] = (_rope(y[:, 1536:2048], cos, sin) * (ATT_DH ** -0.5)).astype(BF16)
    ak_ref[0] = _rope(y[:, 2048:2304], cos, sin).astype(BF16)
    av_ref[0] = y[:, 2304:2560].astype(BF16)
    z = y[:, 2560:2688]
    logits = jnp.dot(z.astype(BF16), w2_ref[...], preferred_element_type=F32) + bg_ref[...]
    g = (jnp.minimum(logits, 0.0) - jnp.log1p(jnp.exp(-jnp.abs(logits)))) * (1.0 / GLA_NORMALIZER)
    g_ref[0, 0] = g[:, 0:256]
    g_ref[1, 0] = g[:, 256:512]


def _inproj(x, mod3, sc_row, sh_row, w_p, w2, bg, cos, sin, tm):
    B, T, D = x.shape
    NP = w_p.shape[1]
    tok = lambda w, dt: jax.ShapeDtypeStruct((B, T, w), dt)
    tspec = lambda w: pl.BlockSpec((1, tm, w), lambda b, t: (b, t, 0))
    return pl.pallas_call(
        _inproj_kernel,
        grid=(B, T // tm),
        in_specs=[tspec(D),
                  pl.BlockSpec((1, 1, D), lambda b, t: (sc_row(b), 0, 0)),
                  pl.BlockSpec((1, 1, D), lambda b, t: (sh_row(b), 0, 0)),
                  pl.BlockSpec((D, NP), lambda b, t: (0, 0)),
                  pl.BlockSpec((LANES, 512), lambda b, t: (0, 0)),
                  pl.BlockSpec((1, 512), lambda b, t: (0, 0)),
                  pl.BlockSpec((tm, LANES), lambda b, t: (t, 0)),
                  pl.BlockSpec((tm, LANES), lambda b, t: (t, 0))],
        out_specs=[tspec(256), tspec(256), tspec(512), tspec(512),
                   pl.BlockSpec((2, 1, tm, 256), lambda b, t: (0, b, t, 0)),
                   tspec(512), tspec(256), tspec(256)],
        out_shape=[tok(256, F32), tok(256, F32), tok(512, BF16), tok(512, F32),
                   jax.ShapeDtypeStruct((2, B, T, 256), F32),
                   tok(512, BF16), tok(256, BF16), tok(256, BF16)],
        compiler_params=_params(("parallel", "parallel")),
        name="inproj",
    )(x, mod3, mod3, w_p, w2, bg, cos, sin)


def _gla_kernel(qf_ref, kf_ref, gf_ref, vf_ref, qb_ref, kb_ref, gb_ref, vb_ref, ck_ref, cg_ref, cv_ref,
                of_ref, ob_ref, st_ref, *, n_chunks, n_ctx_chunks):
    t = pl.program_id(1)
    C = GLA_CHUNK
    npairs = GLA_HEADS // 2
    ri = lax.broadcasted_iota(jnp.int32, (C, C), 0)
    ci = lax.broadcasted_iota(jnp.int32, (C, C), 1)
    tri = [ri >= ci, ri <= ci]
    tri_bf = [m.astype(BF16) for m in tri]
    tri2 = [jnp.concatenate([m, m], axis=0) for m in tri]
    lane = lax.broadcasted_iota(jnp.int32, (1, LANES), 1)
    m0 = (lane < GLA_DK).astype(F32)
    m1 = 1.0 - m0
    rr = lax.broadcasted_iota(jnp.int32, (2 * GLA_DV, LANES), 0)
    ll = lax.broadcasted_iota(jnp.int32, (2 * GLA_DV, LANES), 1)
    blockdiag = (rr < GLA_DV) == (ll < GLA_DK)

    def chunks(items, with_out):
        bs = []
        for d, _, _, gc, _, _ in items:
            ghi = gc.astype(BF16)
            glo = (gc - ghi.astype(F32)).astype(BF16)
            bs.append(jnp.dot(tri_bf[d], ghi, preferred_element_type=F32)
                      + jnp.dot(tri_bf[d], glo, preferred_element_type=F32))
        btots = [jnp.sum(it[3], axis=0, keepdims=True) for it in items]
        sts = [st_ref[it[0], it[1]] for it in items]
        outs = [None] * len(items)
        if with_out:
            qes = [it[5] * jnp.exp(b) for it, b in zip(items, bs)]
            amats = []
            for it, b, qe in zip(items, bs, qes):
                ke = (it[2] * jnp.exp(-b)).astype(BF16)
                lhs = jnp.concatenate([qe * m0, qe * m1], axis=0).astype(BF16)
                amats.append(lax.dot_general(lhs, ke, _NT, preferred_element_type=F32))
            inters = [lax.dot_general(qe.astype(BF16), st.astype(BF16), _NT, preferred_element_type=F32)
                      for qe, st in zip(qes, sts)]
            for n, (it, a) in enumerate(zip(items, amats)):
                a = jnp.where(tri2[it[0]], a, 0.0).astype(BF16)
                oi = jnp.dot(a, it[4], preferred_element_type=F32)
                outs[n] = jnp.concatenate([oi[:C, :GLA_DV], oi[C:, GLA_DV:]], axis=1) + inters[n]
        for it, b, btot, st in zip(items, bs, btots, sts):
            kd = (it[2] * jnp.exp(btot - b)).astype(BF16)
            mt = lax.dot_general(it[4], kd, _TN, preferred_element_type=F32)
            st_ref[it[0], it[1]] = st * jnp.exp(btot) + jnp.where(blockdiag, mt, 0.0)
        return outs

    def qk_lanes(p):
        return slice(p * LANES, (p + 1) * LANES)

    def v_lanes(p):
        return slice(p * 2 * GLA_DV, (p + 1) * 2 * GLA_DV)

    @pl.when(t == 0)
    def _():
        st_ref[...] = jnp.zeros_like(st_ref)
        for j in range(n_ctx_chunks):
            items = []
            for d in range(2):
                cj = j if d == 0 else n_ctx_chunks - 1 - j
                rows = slice(cj * C, (cj + 1) * C)
                items += [(d, p, ck_ref[0, rows, qk_lanes(p)], cg_ref[d, 0, rows, qk_lanes(p)],
                           cv_ref[0, rows, v_lanes(p)], None) for p in range(npairs)]
            chunks(items, False)

    streams = [(qf_ref, kf_ref, gf_ref, vf_ref, of_ref), (qb_ref, kb_ref, gb_ref, vb_ref, ob_ref)]

    def body(j, carry):
        items, dests = [], []
        for d, (q_ref, k_ref, g_ref, v_ref, o_ref) in enumerate(streams):
            cj = j if d == 0 else n_chunks - 1 - j
            rows = pl.ds(pl.multiple_of(cj * C, C), C)
            for p in range(npairs):
                items.append((d, p, k_ref[0, rows, qk_lanes(p)], g_ref[0, 0, rows, qk_lanes(p)],
                              v_ref[0, rows, v_lanes(p)], q_ref[0, rows, qk_lanes(p)]))
                dests.append((o_ref, rows, v_lanes(p)))
        for (o_ref, rows, lanes), out in zip(dests, chunks(items, True)):
            o_ref[0, rows, lanes] = out
        return carry

    lax.fori_loop(0, n_chunks, body, 0)


def _gla(gq, gk, g, gv, ck, cg, cv, tt):
    B, T, W = gq.shape
    L = ck.shape[1]
    nT = T // tt
    VW = gv.shape[2]
    fwd = lambda b, t: (b, t, 0)
    bwd = lambda b, t: (b, nT - 1 - t, 0)
    ctx = lambda b, t: (b, 0, 0)
    out = jax.ShapeDtypeStruct((B, T, VW), F32)
    return pl.pallas_call(
        functools.partial(_gla_kernel, n_chunks=tt // GLA_CHUNK, n_ctx_chunks=L // GLA_CHUNK),
        grid=(B, nT),
        in_specs=[pl.BlockSpec((1, tt, W), fwd), pl.BlockSpec((1, tt, W), fwd),
                  pl.BlockSpec((1, 1, tt, W), lambda b, t: (0, b, t, 0)),
                  pl.BlockSpec((1, tt, VW), fwd),
                  pl.BlockSpec((1, tt, W), bwd), pl.BlockSpec((1, tt, W), bwd),
                  pl.BlockSpec((1, 1, tt, W), lambda b, t: (1, b, nT - 1 - t, 0)),
                  pl.BlockSpec((1, tt, VW), bwd),
                  pl.BlockSpec((1, L, W), ctx),
                  pl.BlockSpec((2, 1, L, W), lambda b, t: (0, b, 0, 0)),
                  pl.BlockSpec((1, L, VW), ctx)],
        out_specs=[pl.BlockSpec((1, tt, VW), fwd), pl.BlockSpec((1, tt, VW), bwd)],
        out_shape=[out, out],
        scratch_shapes=[pltpu.VMEM((2, GLA_HEADS // 2, 2 * GLA_DV, LANES), F32)],
        compiler_params=_params(("parallel", "arbitrary")),
        name="gla",
    )(gq, gk, g, gv, gq, gk, g, gv, ck, cg, cv)


def _attn_kernel(q_ref, kp_ref, kc_ref, kn_ref, vp_ref, vc_ref, vn_ref, ck_ref, cv_ref, sink_ref,
                 o_ref, *, n_blocks):
    n = pl.program_id(1)
    BLK = ATT_BLOCK
    L = ck_ref.shape[1]
    nk = 3 * BLK + L
    group = ATT_HEADS // ATT_KV_HEADS
    ki = lax.broadcasted_iota(jnp.int32, (nk, group * BLK), 0)
    qi = lax.broadcasted_iota(jnp.int32, (nk, group * BLK), 1) & (BLK - 1)
    band = jnp.abs(ki - BLK - qi) <= WINDOW
    prev_ok = jnp.logical_or(ki >= BLK, n > 0)
    next_ok = jnp.logical_or(ki < 2 * BLK, n < n_blocks - 1)
    valid = jnp.logical_or(ki >= 3 * BLK, band & prev_ok & next_ok)
    lane = lax.broadcasted_iota(jnp.int32, (BLK, LANES), 1)
    zero = jnp.zeros((), BF16)
    for kv in range(ATT_KV_HEADS):
        ks = slice(kv * LANES, (kv + 1) * LANES)
        kcat = jnp.concatenate([kp_ref[0, :, ks], kc_ref[0, :, ks], kn_ref[0, :, ks], ck_ref[0, :, ks]], axis=0)
        vcat = jnp.concatenate([vp_ref[0, :, ks], vc_ref[0, :, ks], vn_ref[0, :, ks], cv_ref[0, :, ks]], axis=0)
        qparts = []
        for gi in range(group // 2):
            q2 = q_ref[0, :, (kv * group // 2 + gi) * LANES:(kv * group // 2 + gi + 1) * LANES]
            qparts += [jnp.where(lane < ATT_DH, q2, zero), jnp.where(lane >= ATT_DH, q2, zero)]
        qstack = jnp.concatenate(qparts, axis=0)
        s = lax.dot_general(kcat, qstack, _NT, preferred_element_type=F32)
        s = jnp.where(valid, s, NEG_INF)
        snk = jnp.concatenate([sink_ref[kv * group + hh:kv * group + hh + 1, :] for hh in range(group)], axis=1)
        m = jnp.maximum(jnp.max(s, axis=0, keepdims=True), snk)
        p = jnp.exp(s - m)
        den = jnp.sum(p, axis=0, keepdims=True) + jnp.exp(snk - m)
        ot = lax.dot_general(vcat, p.astype(BF16), _TN, preferred_element_type=F32)
        ot = ot[0:ATT_DH] / den
        for gi in range(group // 2):
            pair = jnp.concatenate([ot[:, (2 * gi) * BLK:(2 * gi + 1) * BLK],
                                    ot[:, (2 * gi + 1) * BLK:(2 * gi + 2) * BLK]], axis=0)
            cols = slice((kv * group // 2 + gi) * LANES, (kv * group // 2 + gi + 1) * LANES)
            o_ref[0, :, cols] = pair.T.astype(o_ref.dtype)


def _attn(aq, ak, av, ck, cv, sink_bc):
    B, T, _ = aq.shape
    L = ck.shape[1]
    nb = T // ATT_BLOCK
    kvw = ak.shape[2]
    cur = lambda b, n: (b, n, 0)
    prev = lambda b, n: (b, jnp.maximum(n - 1, 0), 0)
    nxt = lambda b, n: (b, jnp.minimum(n + 1, nb - 1), 0)
    kspec = lambda im: pl.BlockSpec((1, ATT_BLOCK, kvw), im)
    return pl.pallas_call(
        functools.partial(_attn_kernel, n_blocks=nb),
        grid=(B, nb),
        in_specs=[pl.BlockSpec((1, ATT_BLOCK, ATT_HEADS * ATT_DH), cur),
                  kspec(prev), kspec(cur), kspec(nxt), kspec(prev), kspec(cur), kspec(nxt),
                  pl.BlockSpec((1, L, kvw), lambda b, n: (b, 0, 0)),
                  pl.BlockSpec((1, L, kvw), lambda b, n: (b, 0, 0)),
                  pl.BlockSpec((ATT_HEADS, LANES), lambda b, n: (0, 0))],
        out_specs=pl.BlockSpec((1, ATT_BLOCK, ATT_HEADS * ATT_DH), cur),
        out_shape=jax.ShapeDtypeStruct((B, T, ATT_HEADS * ATT_DH), BF16),
        compiler_params=_params(("parallel", "parallel")),
        name="attn",
    )(aq, ak, ak, ak, av, av, av, ck, cv, sink_bc)


def _layernorm(v, g, b):
    mu = jnp.mean(v, axis=-1, keepdims=True)
    vc = v - mu
    var = jnp.mean(vc * vc, axis=-1, keepdims=True)
    return vc * lax.rsqrt(var + LN_EPS) * g + b


def _outproj_kernel(of_ref, ob_ref, r_ref, att_ref, x_ref, g1_ref, sh2_ref, sc2_ref,
                    wout_ref, gng_ref, lng_ref, lnb_ref, wqt_ref, sk_ref,
                    x1_ref, hpt_ref, st_ref, *, alpha):
    o = of_ref[0] + ob_ref[0]
    r = r_ref[0]
    gng = gng_ref[...]
    parts = []
    for h in range(GLA_HEADS):
        oh = o[:, h * GLA_DV:(h + 1) * GLA_DV]
        oh = oh * lax.rsqrt(jnp.mean(oh * oh, axis=-1, keepdims=True) + LN_EPS) * gng
        rh = r[:, h * GLA_DV:(h + 1) * GLA_DV]
        parts.append((oh * (rh * jax.nn.sigmoid(rh))).astype(BF16))
    parts.append(att_ref[0])
    cat = jnp.concatenate(parts, axis=1)
    y = jnp.dot(cat, wout_ref[...], preferred_element_type=F32)
    x1 = _layernorm(alpha * x_ref[0] + g1_ref[0] * y, lng_ref[...], lnb_ref[...])
    x1_ref[0] = x1
    hp = x1 * (1.0 + sc2_ref[0]) + sh2_ref[0]
    hpt = hp.T.astype(BF16)
    hpt_ref[0, 0] = hpt
    qt = jnp.dot(wqt_ref[...], hpt, preferred_element_type=F32)
    for h in range(PEER_HEADS):
        qh = qt[h * PEER_DQ:(h + 1) * PEER_DQ, :].astype(BF16)
        s = jnp.dot(sk_ref[h], qh, preferred_element_type=F32)
        st_ref[0, h] = s


def _outproj(of, ob, r, att, x, mod3, w_out, gng, ln_g, ln_b, wqt, skb, alpha, tm):
    B, T, D = x.shape
    W = of.shape[2]
    tok = lambda w: pl.BlockSpec((1, tm, w), lambda b, t: (b, t, 0))
    modrow = lambda j: pl.BlockSpec((1, 1, D), lambda b, t: (b * N_MOD + j, 0, 0))
    full = lambda a: pl.BlockSpec(a.shape, lambda b, t: (0,) * a.ndim)
    return pl.pallas_call(
        functools.partial(_outproj_kernel, alpha=alpha),
        grid=(B, T // tm),
        in_specs=[tok(W), tok(W), tok(W), tok(att.shape[2]), tok(D), modrow(2), modrow(3), modrow(4),
                  full(w_out), full(gng), full(ln_g), full(ln_b), full(wqt), full(skb)],
        out_specs=[tok(D),
                   pl.BlockSpec((1, 1, D, tm), lambda b, t: (b, t, 0, 0)),
                   pl.BlockSpec((1, PEER_HEADS, 2 * N_KEYS, tm), lambda b, t: (b, 0, 0, t))],
        out_shape=[jax.ShapeDtypeStruct((B, T, D), F32),
                   jax.ShapeDtypeStruct((B, T // tm, D, tm), BF16),
                   jax.ShapeDtypeStruct((B, PEER_HEADS, 2 * N_KEYS, T), F32)],
        compiler_params=_params(("parallel", "parallel")),
        name="outproj",
    )(of, ob, r, att, x, mod3, mod3, mod3, w_out, gng, ln_g, ln_b, wqt, skb)


def _sort_network(n):
    pairs = []

    def merge(lo, hi, r):
        step = 2 * r
        if step < hi - lo:
            merge(lo, hi, step)
            merge(lo + r, hi, step)
            pairs.extend((i, i + r) for i in range(lo + r, hi - r, step))
        else:
            pairs.append((lo, lo + r))

    def sort(lo, hi):
        if hi - lo >= 1:
            mid = lo + (hi - lo) // 2
            sort(lo, mid)
            sort(mid + 1, hi)
            merge(lo, hi, 1)

    sort(0, n - 1)
    return pairs


_SLABS = 16
_NET = _sort_network(_SLABS)


def _largest_desc(slabs, count):
    s = list(slabs) + [None] * (_SLABS - len(slabs))
    for i, j in _NET:
        if s[j] is None:
            continue
        if s[i] is None:
            s[i], s[j] = s[j], None
        else:
            s[i], s[j] = jnp.maximum(s[i], s[j]), jnp.minimum(s[i], s[j])
    depth = len(slabs)
    assert all(x is not None for x in s[:depth]) and all(x is None for x in s[depth:])
    vals = []
    for r in range(count):
        m = jnp.max(s[0], axis=0, keepdims=True)
        vals.append(m)
        hit = s[0] == m
        for l in range(min(depth, count - r - 1)):
            s[l] = jnp.where(hit, s[l + 1] if l + 1 < depth else NEG_INF, s[l])
    return vals


def _slabs(v):
    return [v[8 * k:8 * (k + 1)] for k in range(v.shape[0] // 8)]


def _topk_kernel(s_ref, g_ref, s2_ref, kap_ref):
    K = PEER_TOPK
    s1 = s_ref[0, 0, 0:N_KEYS, :]
    s2 = s_ref[0, 0, N_KEYS:2 * N_KEYS, :]
    a = _largest_desc(_slabs(s1), K + 1)
    b = _largest_desc(_slabs(s2), K + 1)
    bmat = jnp.concatenate(b[:K], axis=0)
    row8 = lax.broadcasted_iota(jnp.int32, (8, s1.shape[1]), 0)
    cands = _slabs(a[0] + bmat)
    for r in range(1, K // 2):
        cands.append(jnp.where(row8 < (K + 1) // (r + 1), a[r] + bmat[0:8], NEG_INF))
    cands.append(jnp.concatenate(a[K // 2:K], axis=0) + b[0])
    cands.append(jnp.where(row8 == 0, a[0] + b[K], jnp.where(row8 == 1, a[K] + b[0], NEG_INF)))
    top = _largest_desc(cands, K + 1)
    tau = 0.5 * (top[K - 1] + top[K])
    mx = top[0]
    zsum = top[0] * 0.0
    for r in range(K):
        zsum = zsum + jnp.exp(top[r] - mx)
    shift = mx + jnp.log(zsum)
    g = (s1 - shift) * LOG2E
    s2l = s2 * LOG2E
    kap = (tau - shift) * LOG2E
    for ci in range(s1.shape[1] // LANES):
        cols = slice(ci * LANES, (ci + 1) * LANES)
        g_ref[0, 0, ci] = g[:, cols]
        s2_ref[0, 0, ci] = s2l[:, cols]
        kap_ref[0, 0, ci] = kap[:, cols]


def _topk(st, tp):
    B, H, _, T = st.shape
    out = jax.ShapeDtypeStruct((B, H, T // LANES, N_KEYS, LANES), F32)
    ospec = pl.BlockSpec((1, 1, tp // LANES, N_KEYS, LANES), lambda b, h, t: (b, h, t, 0, 0))
    return pl.pallas_call(
        _topk_kernel,
        grid=(B, H, T // tp),
        in_specs=[pl.BlockSpec((1, 1, 2 * N_KEYS, tp), lambda b, h, t: (b, h, 0, t))],
        out_specs=[ospec, ospec,
                   pl.BlockSpec((1, 1, tp // LANES, 1, LANES), lambda b, h, t: (b, h, t, 0, 0))],
        out_shape=[out, out, jax.ShapeDtypeStruct((B, H, T // LANES, 1, LANES), F32)],
        compiler_params=_params(("parallel", "parallel", "parallel")),
        name="topk",
    )(st)


def _peer_kernel(hpt_ref, u_ref, un_ref, vt_ref, vp_ref, s2_ref, g_ref, kap_ref, x1_ref, g2_ref,
                 lng_ref, lnb_ref, o_ref, acc_ref, z0_ref, z1_ref, a0_ref, a1_ref, *, alpha, n_steps):
    e = pl.program_id(2)
    n_chunks = acc_ref.shape[0]
    te = u_ref.shape[0]
    n_i = te // N_KEYS
    sub = PEER_CHUNK // LANES
    upper = (e * n_i) % 8 >= n_i

    @pl.when(e == 0)
    def _():
        acc_ref[...] = jnp.zeros_like(acc_ref)
        a1_ref[...] = jnp.zeros_like(a1_ref)
        z0_ref[...] = jnp.dot(u_ref[...], hpt_ref[0, 0], preferred_element_type=F32)

    def chunk(k, z_cur, a_cur, z_next, a_prev):
        def next_scores():
            if k + 1 < n_chunks:
                z_next[...] = jnp.dot(u_ref[...], hpt_ref[0, k + 1], preferred_element_type=F32)
            else:
                z_next[...] = jnp.dot(un_ref[...], hpt_ref[0, 0], preferred_element_type=F32)

        def prev_output():
            if k > 0:
                acc_ref[k - 1] += jnp.dot(vt_ref[...], a_prev[...], preferred_element_type=F32)
            else:
                acc_ref[n_chunks - 1] += jnp.dot(vp_ref[...], a_prev[...], preferred_element_type=F32)

        def tile(ii, cc):
            rows = slice(ii * N_KEYS, (ii + 1) * N_KEYS)
            cols = slice(cc * LANES, (cc + 1) * LANES)
            c = k * sub + cc
            w = jnp.zeros((N_KEYS, LANES), F32)
            for h in range(PEER_HEADS):
                grow = g_ref[0, h, c, ii:ii + 1, :]
                if 2 * n_i <= 8:
                    grow = jnp.where(upper, g_ref[0, h, c, n_i + ii:n_i + ii + 1, :], grow)
                t = s2_ref[0, h, c] + grow
                w = w + jnp.exp2(jnp.where(t >= kap_ref[0, h, c], t, NEG_INF))
            zt = z_cur[rows, cols]
            act = 0.5 * zt * (1.0 + lax.erf(zt * (2.0 ** -0.5)))
            a_cur[rows, cols] = (w * act).astype(BF16)

        tiles = [(ii, cc) for ii in range(n_i) for cc in range(sub)]
        prev_output()
        for ii, cc in tiles[:len(tiles) // 2]:
            tile(ii, cc)
        next_scores()
        for ii, cc in tiles[len(tiles) // 2:]:
            tile(ii, cc)

    for k in range(n_chunks):
        if k % 2 == 0:
            chunk(k, z0_ref, a0_ref, z1_ref, a1_ref)
        else:
            chunk(k, z1_ref, a1_ref, z0_ref, a0_ref)

    @pl.when(e == n_steps - 1)
    def _():
        acc_ref[n_chunks - 1] += jnp.dot(vt_ref[...], a1_ref[...], preferred_element_type=F32)
        for k in range(n_chunks):
            rows = slice(k * PEER_CHUNK, (k + 1) * PEER_CHUNK)
            y = acc_ref[k].T
            o_ref[0, rows, :] = _layernorm(alpha * x1_ref[0, rows, :] + g2_ref[0] * y,
                                           lng_ref[...], lnb_ref[...])


def _peer(hpt, u, vt, s2, g, kap, x1, mod3, ln_g, ln_b, alpha, tm, te):
    B, _, D, _ = hpt.shape
    T = x1.shape[1]
    E = u.shape[0]
    n_steps = E // te
    nc = tm // LANES
    nk = tm // PEER_CHUNK
    side = pl.BlockSpec((1, PEER_HEADS, nc, N_KEYS, LANES), lambda b, t, e: (b, 0, t, 0, 0))
    n_i = te // N_KEYS
    assert n_i in (4, 8) and nk % 2 == 0
    keyrows = pl.BlockSpec((1, PEER_HEADS, nc, 8, LANES), lambda b, t, e: (b, 0, t, (e * n_i) // 8, 0))
    return pl.pallas_call(
        functools.partial(_peer_kernel, alpha=alpha, n_steps=n_steps),
        grid=(B, T // tm, n_steps),
        in_specs=[pl.BlockSpec((1, nk, D, PEER_CHUNK), lambda b, t, e: (b, t, 0, 0)),
                  pl.BlockSpec((te, D), lambda b, t, e: (e, 0)),
                  pl.BlockSpec((te, D), lambda b, t, e: (jnp.minimum(e + 1, n_steps - 1), 0)),
                  pl.BlockSpec((D, te), lambda b, t, e: (0, e)),
                  pl.BlockSpec((D, te), lambda b, t, e: (0, jnp.maximum(e - 1, 0))),
                  side, keyrows,
                  pl.BlockSpec((1, PEER_HEADS, nc, 1, LANES), lambda b, t, e: (b, 0, t, 0, 0)),
                  pl.BlockSpec((1, tm, D), lambda b, t, e: (b, t, 0)),
                  pl.BlockSpec((1, 1, D), lambda b, t, e: (b * N_MOD + 5, 0, 0)),
                  pl.BlockSpec((1, D), lambda b, t, e: (0, 0)),
                  pl.BlockSpec((1, D), lambda b, t, e: (0, 0))],
        out_specs=pl.BlockSpec((1, tm, D), lambda b, t, e: (b, t, 0)),
        out_shape=jax.ShapeDtypeStruct((B, T, D), F32),
        scratch_shapes=[pltpu.VMEM((nk, D, PEER_CHUNK), F32),
                        pltpu.VMEM((te, PEER_CHUNK), F32), pltpu.VMEM((te, PEER_CHUNK), F32),
                        pltpu.VMEM((te, PEER_CHUNK), BF16), pltpu.VMEM((te, PEER_CHUNK), BF16)],
        compiler_params=_params(("parallel", "parallel", "arbitrary")),
        name="peer",
    )(hpt, u, u, vt, vt, s2, g, kap, x1, mod3, ln_g, ln_b)


def _rope_tables(n_tokens):
    rows = n_tokens // GRID_W
    row = jnp.repeat(jnp.arange(rows, dtype=F32), GRID_W)
    col = jnp.tile(jnp.arange(GRID_W, dtype=F32), rows)
    n_freq = ATT_DH // 4
    inv = ROPE_THETA ** (-jnp.arange(n_freq, dtype=F32) / n_freq)
    ar, ac = row[:, None] * inv, col[:, None] * inv
    cos = jnp.concatenate([jnp.cos(ar), jnp.cos(ar), jnp.cos(ac), jnp.cos(ac)], axis=1)
    sin = jnp.concatenate([-jnp.sin(ar), jnp.sin(ar), -jnp.sin(ac), jnp.sin(ac)], axis=1)
    return jnp.tile(cos, (1, 2)), jnp.tile(sin, (1, 2))


def _dup_heads(w, n_heads, dh):
    return jnp.concatenate([w[:, h * dh:(h + 1) * dh] for h in range(n_heads) for _ in range(2)], axis=1)


def kernel(x, c, ctx, c_ctx, w_ada, b_ada, w_in, w_gate2_f, b_gate_f, w_gate2_b, b_gate_b, gla_norm_g,
           attn_sink, w_out, ln1_g, ln1_b, peer_wq, peer_subkeys, peer_u, peer_v, ln2_g, ln2_b):
    B, T, D = x.shape
    L = ctx.shape[1]
    depth = w_ada.shape[0]
    assert depth == 1, "single-layer block only"
    alpha = (2.0 * depth) ** 0.25

    cond = jnp.concatenate([jax.nn.silu(c), jax.nn.silu(c_ctx)[None]], axis=0)
    cond_bc = jnp.broadcast_to(cond[:, :, None], (B + 1, D, LANES))
    mod = _ada(cond_bc, w_ada[0], b_ada[0][None])
    mod3 = mod.reshape(8 * N_MOD, 1, D)

    wi = w_in[0]
    kq = 2 * GLA_HEADS * GLA_DK + 2 * GLA_HEADS * GLA_DV
    z0, a0 = kq, kq + 2 * GLA_RANK
    k0 = a0 + ATT_HEADS * ATT_DH
    v0 = k0 + ATT_KV_HEADS * ATT_DH
    w_p = jnp.concatenate([
        wi[:, :kq], wi[:, a0:k0],
        _dup_heads(wi[:, k0:v0], ATT_KV_HEADS, ATT_DH),
        _dup_heads(wi[:, v0:], ATT_KV_HEADS, ATT_DH),
        wi[:, z0:a0], jnp.zeros((D, LANES - 2 * GLA_RANK), F32)], axis=1).astype(BF16)
    gw = GLA_HEADS * GLA_DK
    w2 = jnp.zeros((LANES, 2 * gw), F32)
    w2 = w2.at[0:GLA_RANK, 0:gw].set(w_gate2_f[0]).at[GLA_RANK:2 * GLA_RANK, gw:].set(w_gate2_b[0]).astype(BF16)
    bg = jnp.concatenate([b_gate_f[0], b_gate_b[0]])[None]
    cos, sin = _rope_tables(T)
    ones = jnp.ones((L, LANES), F32)

    gq, gk, gv, r, g, aq, ak, av = _inproj(x, mod3, lambda b: b * N_MOD + 1, lambda b: b * N_MOD,
                                           w_p, w2, bg, cos, sin, tm=min(T, 512))
    _, cgk, cgv, _, cg, _, cak, cav = _inproj(ctx, mod3, lambda b: B * N_MOD + 1, lambda b: B * N_MOD,
                                              w_p, w2, bg, ones, 0.0 * ones, tm=L)

    of, ob = _gla(gq, gk, g, gv, cgk, cg, cgv, tt=min(T, 512))
    sink_bc = jnp.broadcast_to(attn_sink[0][:, None], (ATT_HEADS, LANES))
    att = _attn(aq, ak, av, cak, cav, sink_bc)

    sk = peer_subkeys[0]
    half = PEER_DQ // 2
    skb = jnp.zeros((PEER_HEADS, 2 * N_KEYS, PEER_DQ), F32)
    skb = skb.at[:, :N_KEYS, :half].set(sk[:, 0]).at[:, N_KEYS:, half:].set(sk[:, 1])
    skb = skb.astype(BF16)
    x1, hpt, st = _outproj(of, ob, r, att, x, mod3, w_out[0].astype(BF16), gla_norm_g[0][None],
                           ln1_g[0][None], ln1_b[0][None], peer_wq[0].T.astype(BF16), skb,
                           alpha, tm=PEER_CHUNK)

    g, s2, kap = _topk(st, tp=min(T, 512))
    u = peer_u[0].astype(BF16)
    vt = peer_v[0].T.astype(BF16)
    return _peer(hpt, u, vt, s2, g, kap, x1, mod3, ln2_g[0][None], ln2_b[0][None],
                 alpha, tm=min(T, 1024), te=512)
```

```python
import functools

import jax
import jax.numpy as jnp
from jax import lax
from jax.experimental import pallas as pl
from jax.experimental.pallas import tpu as pltpu

F32 = jnp.float32
BF16 = jnp.bfloat16

GRID_W = 64
GLA_HEADS = 4
GLA_DK = 64
GLA_DV = 128
GLA_RANK = 16
GLA_NORMALIZER = 16.0
GLA_CHUNK = 64
ATT_HEADS = 8
ATT_KV_HEADS = 2
ATT_DH = 64
WINDOW = 128
ATT_BLOCK = 128
ROPE_THETA = 10000.0
PEER_HEADS = 8
N_KEYS = 128
PEER_DQ = 128
PEER_TOPK = 16
N_MOD = 6
LN_EPS = 1e-5
NEG_INF = -1e30
LANES = 128
PEER_CHUNK = 256
LOG2E = 1.4426950408889634
MXU_TILE = 256

VMEM_LIMIT = 56 * 1024 * 1024

_NT = (((1,), (1,)), ((), ()))
_TN = (((0,), (0,)), ((), ()))


def _params(sem):
    return pltpu.CompilerParams(dimension_semantics=sem, vmem_limit_bytes=VMEM_LIMIT)


def _ada_kernel(s_ref, w_ref, b_ref, o_ref, *, n_rows):
    tn = w_ref.shape[1]
    w = w_ref[...]
    rows = []
    for r in range(n_rows):
        s = s_ref[r]
        cols = [jnp.sum(w[:, j * LANES:(j + 1) * LANES] * s, axis=0, keepdims=True)
                for j in range(tn // LANES)]
        rows.append(jnp.concatenate(cols, axis=1) + b_ref[...])
    rows.append(jnp.zeros((8 - n_rows, tn), F32))
    o_ref[...] = jnp.concatenate(rows, axis=0)


def _ada(cond_bc, w, b):
    n_rows, d, _ = cond_bc.shape
    n = w.shape[1]
    tn = 512
    return pl.pallas_call(
        functools.partial(_ada_kernel, n_rows=n_rows),
        grid=(n // tn,),
        in_specs=[pl.BlockSpec((n_rows, d, LANES), lambda j: (0, 0, 0)),
                  pl.BlockSpec((d, tn), lambda j: (0, j)),
                  pl.BlockSpec((1, tn), lambda j: (0, j))],
        out_specs=pl.BlockSpec((8, tn), lambda j: (0, j)),
        out_shape=jax.ShapeDtypeStruct((8, n), F32),
        compiler_params=_params(("parallel",)),
        name="ada",
    )(cond_bc, w, b)


def _rope(a, cos, sin):
    n = a.shape[1] // LANES
    c = jnp.concatenate([cos] * n, axis=1)
    s = jnp.concatenate([sin] * n, axis=1)
    lane = lax.broadcasted_iota(jnp.int32, a.shape, 1)
    swapped = jnp.where((lane % 32) < 16,
                        pltpu.roll(a, a.shape[1] - 16, 1),
                        pltpu.roll(a, 16, 1))
    return a * c + swapped * s


def _inproj_kernel(x_ref, sc_ref, sh_ref, w_ref, w2_ref, bg_ref, cos_ref, sin_ref,
                   gq_ref, gk_ref, gv_ref, r_ref, g_ref, aq_ref, ak_ref, av_ref):
    h = x_ref[0] * (1.0 + sc_ref[0]) + sh_ref[0]
    y = jnp.dot(h.astype(BF16), w_ref[...], preferred_element_type=F32)
    gq_ref[0] = y[:, 0:256] * (GLA_DK ** -0.5)
    gk_ref[0] = y[:, 256:512]
    gv_ref[0] = y[:, 512:1024].astype(BF16)
    r_ref[0] = y[:, 1024:1536]
    cos = cos_ref[...]
    sin = sin_ref[...]
    aq_ref[0] = (_rope(y[:, 1536:2048], cos, sin) * (ATT_DH ** -0.5)).astype(BF16)
    ak_ref[0] = _rope(y[:, 2048:2304], cos, sin).astype(BF16)
    av_ref[0] = y[:, 2304:2560].astype(BF16)
    z = y[:, 2560:2688]
    logits = jnp.dot(z.astype(BF16), w2_ref[...], preferred_element_type=F32) + bg_ref[...]
    g = (jnp.minimum(logits, 0.0) - jnp.log1p(jnp.exp(-jnp.abs(logits)))) * (1.0 / GLA_NORMALIZER)
    g_ref[0, 0] = g[:, 0:256]
    g_ref[1, 0] = g[:, 256:512]


def _inproj(x, mod3, sc_row, sh_row, w_p, w2, bg, cos, sin, tm):
    B, T, D = x.shape
    NP = w_p.shape[1]
    tok = lambda w, dt: jax.ShapeDtypeStruct((B, T, w), dt)
    tspec = lambda w: pl.BlockSpec((1, tm, w), lambda b, t: (b, t, 0))
    return pl.pallas_call(
        _inproj_kernel,
        grid=(B, T // tm),
        in_specs=[tspec(D),
                  pl.BlockSpec((1, 1, D), lambda b, t: (sc_row(b), 0, 0)),
                  pl.BlockSpec((1, 1, D), lambda b, t: (sh_row(b), 0, 0)),
                  pl.BlockSpec((D, NP), lambda b, t: (0, 0)),
                  pl.BlockSpec((LANES, 512), lambda b, t: (0, 0)),
                  pl.BlockSpec((1, 512), lambda b, t: (0, 0)),
                  pl.BlockSpec((tm, LANES), lambda b, t: (t, 0)),
                  pl.BlockSpec((tm, LANES), lambda b, t: (t, 0))],
        out_specs=[tspec(256), tspec(256), tspec(512), tspec(512),
                   pl.BlockSpec((2, 1, tm, 256), lambda b, t: (0, b, t, 0)),
                   tspec(512), tspec(256), tspec(256)],
        out_shape=[tok(256, F32), tok(256, F32), tok(512, BF16), tok(512, F32),
                   jax.ShapeDtypeStruct((2, B, T, 256), F32),
                   tok(512, BF16), tok(256, BF16), tok(256, BF16)],
        compiler_params=_params(("parallel", "parallel")),
        name="inproj",
    )(x, mod3, mod3, w_p, w2, bg, cos, sin)


def _gla_kernel(qf_ref, kf_ref, gf_ref, vf_ref, qb_ref, kb_ref, gb_ref, vb_ref, ck_ref, cg_ref, cv_ref,
                of_ref, ob_ref, st_ref, *, n_chunks, n_ctx_chunks):
    t = pl.program_id(0)
    nb = qf_ref.shape[0]
    C = GLA_CHUNK
    npairs = GLA_HEADS // 2
    ri = lax.broadcasted_iota(jnp.int32, (C, C), 0)
    ci = lax.broadcasted_iota(jnp.int32, (C, C), 1)
    tri = [ri >= ci, ri <= ci]
    tri_bf = [m.astype(BF16) for m in tri]
    tri2 = [jnp.concatenate([m, m], axis=0) for m in tri]
    lane = lax.broadcasted_iota(jnp.int32, (1, LANES), 1)
    m0 = (lane < GLA_DK).astype(F32)
    m1 = 1.0 - m0
    rr = lax.broadcasted_iota(jnp.int32, (2 * GLA_DV, LANES), 0)
    ll = lax.broadcasted_iota(jnp.int32, (2 * GLA_DV, LANES), 1)
    blockdiag = (rr < GLA_DV) == (ll < GLA_DK)

    def chunks(items, with_out):
        bs = []
        for (_, d, _), _, gc, _, _ in items:
            ghi = gc.astype(BF16)
            glo = (gc - ghi.astype(F32)).astype(BF16)
            bs.append(jnp.dot(tri_bf[d], ghi, preferred_element_type=F32)
                      + jnp.dot(tri_bf[d], glo, preferred_element_type=F32))
        btots = [jnp.sum(it[2], axis=0, keepdims=True) for it in items]
        sts = [st_ref[it[0]] for it in items]
        outs = [None] * len(items)
        if with_out:
            qes = [it[4] * jnp.exp(b) for it, b in zip(items, bs)]
            amats = []
            for it, b, qe in zip(items, bs, qes):
                ke = (it[1] * jnp.exp(-b)).astype(BF16)
                lhs = jnp.concatenate([qe * m0, qe * m1], axis=0).astype(BF16)
                amats.append(lax.dot_general(lhs, ke, _NT, preferred_element_type=F32))
            inters = [lax.dot_general(qe.astype(BF16), st.astype(BF16), _NT, preferred_element_type=F32)
                      for qe, st in zip(qes, sts)]
            for n, (it, a) in enumerate(zip(items, amats)):
                a = jnp.where(tri2[it[0][1]], a, 0.0).astype(BF16)
                oi = jnp.dot(a, it[3], preferred_element_type=F32)
                outs[n] = jnp.concatenate([oi[:C, :GLA_DV], oi[C:, GLA_DV:]], axis=1) + inters[n]
        for it, b, btot, st in zip(items, bs, btots, sts):
            kd = (it[1] * jnp.exp(btot - b)).astype(BF16)
            mt = lax.dot_general(it[3], kd, _TN, preferred_element_type=F32)
            st_ref[it[0]] = st * jnp.exp(btot) + jnp.where(blockdiag, mt, 0.0)
        return outs

    def qk_lanes(p):
        return slice(p * LANES, (p + 1) * LANES)

    def v_lanes(p):
        return slice(p * 2 * GLA_DV, (p + 1) * 2 * GLA_DV)

    @pl.when(t == 0)
    def _():
        st_ref[...] = jnp.zeros_like(st_ref)
        for j in range(n_ctx_chunks):
            items = []
            for bi in range(nb):
                for d in range(2):
                    cj = j if d == 0 else n_ctx_chunks - 1 - j
                    rows = slice(cj * C, (cj + 1) * C)
                    items += [((bi, d, p), ck_ref[bi, rows, qk_lanes(p)], cg_ref[d, bi, rows, qk_lanes(p)],
                               cv_ref[bi, rows, v_lanes(p)], None) for p in range(npairs)]
            chunks(items, False)

    streams = [(qf_ref, kf_ref, gf_ref, vf_ref, of_ref), (qb_ref, kb_ref, gb_ref, vb_ref, ob_ref)]

    def body(j, carry):
        items, dests = [], []
        for bi in range(nb):
            for d, (q_ref, k_ref, g_ref, v_ref, o_ref) in enumerate(streams):
                cj = j if d == 0 else n_chunks - 1 - j
                rows = pl.ds(pl.multiple_of(cj * C, C), C)
                for p in range(npairs):
                    items.append(((bi, d, p), k_ref[bi, rows, qk_lanes(p)], g_ref[0, bi, rows, qk_lanes(p)],
                                  v_ref[bi, rows, v_lanes(p)], q_ref[bi, rows, qk_lanes(p)]))
                    dests.append((o_ref, bi, rows, v_lanes(p)))
        for (o_ref, bi, rows, lanes), out in zip(dests, chunks(items, True)):
            o_ref[bi, rows, lanes] = out
        return carry

    lax.fori_loop(0, n_chunks, body, 0)


def _gla(gq, gk, g, gv, ck, cg, cv, tt):
    B, T, W = gq.shape
    L = ck.shape[1]
    nT = T // tt
    VW = gv.shape[2]
    fwd = lambda t: (0, t, 0)
    bwd = lambda t: (0, nT - 1 - t, 0)
    ctx = lambda t: (0, 0, 0)
    out = jax.ShapeDtypeStruct((B, T, VW), F32)
    return pl.pallas_call(
        functools.partial(_gla_kernel, n_chunks=tt // GLA_CHUNK, n_ctx_chunks=L // GLA_CHUNK),
        grid=(nT,),
        in_specs=[pl.BlockSpec((B, tt, W), fwd), pl.BlockSpec((B, tt, W), fwd),
                  pl.BlockSpec((1, B, tt, W), lambda t: (0, 0, t, 0)),
                  pl.BlockSpec((B, tt, VW), fwd),
                  pl.BlockSpec((B, tt, W), bwd), pl.BlockSpec((B, tt, W), bwd),
                  pl.BlockSpec((1, B, tt, W), lambda t: (1, 0, nT - 1 - t, 0)),
                  pl.BlockSpec((B, tt, VW), bwd),
                  pl.BlockSpec((B, L, W), ctx),
                  pl.BlockSpec((2, B, L, W), lambda t: (0, 0, 0, 0)),
                  pl.BlockSpec((B, L, VW), ctx)],
        out_specs=[pl.BlockSpec((B, tt, VW), fwd), pl.BlockSpec((B, tt, VW), bwd)],
        out_shape=[out, out],
        scratch_shapes=[pltpu.VMEM((B, 2, GLA_HEADS // 2, 2 * GLA_DV, LANES), F32)],
        compiler_params=_params(("arbitrary",)),
        name="gla",
    )(gq, gk, g, gv, gq, gk, g, gv, ck, cg, cv)


def _attn_kernel(q_ref, kp_ref, kc_ref, kn_ref, vp_ref, vc_ref, vn_ref, ck_ref, cv_ref, sink_ref,
                 o_ref, *, n_blocks):
    n = pl.program_id(1)
    BLK = ATT_BLOCK
    L = ck_ref.shape[1]
    nk = 3 * BLK + L
    group = ATT_HEADS // ATT_KV_HEADS
    ki = lax.broadcasted_iota(jnp.int32, (nk, group * BLK), 0)
    qi = lax.broadcasted_iota(jnp.int32, (nk, group * BLK), 1) & (BLK - 1)
    band = jnp.abs(ki - BLK - qi) <= WINDOW
    prev_ok = jnp.logical_or(ki >= BLK, n > 0)
    next_ok = jnp.logical_or(ki < 2 * BLK, n < n_blocks - 1)
    valid = jnp.logical_or(ki >= 3 * BLK, band & prev_ok & next_ok)
    lane = lax.broadcasted_iota(jnp.int32, (BLK, LANES), 1)
    zero = jnp.zeros((), BF16)
    for kv in range(ATT_KV_HEADS):
        ks = slice(kv * LANES, (kv + 1) * LANES)
        kcat = jnp.concatenate([kp_ref[0, :, ks], kc_ref[0, :, ks], kn_ref[0, :, ks], ck_ref[0, :, ks]], axis=0)
        vcat = jnp.concatenate([vp_ref[0, :, ks], vc_ref[0, :, ks], vn_ref[0, :, ks], cv_ref[0, :, ks]], axis=0)
        qparts = []
        for gi in range(group // 2):
            q2 = q_ref[0, :, (kv * group // 2 + gi) * LANES:(kv * group // 2 + gi + 1) * LANES]
            qparts += [jnp.where(lane < ATT_DH, q2, zero), jnp.where(lane >= ATT_DH, q2, zero)]
        qstack = jnp.concatenate(qparts, axis=0)
        s = lax.dot_general(kcat, qstack, _NT, preferred_element_type=F32)
        s = jnp.where(valid, s, NEG_INF)
        snk = jnp.concatenate([sink_ref[kv * group + hh:kv * group + hh + 1, :] for hh in range(group)], axis=1)
        m = jnp.maximum(jnp.max(s, axis=0, keepdims=True), snk)
        p = jnp.exp(s - m)
        den = jnp.sum(p, axis=0, keepdims=True) + jnp.exp(snk - m)
        ot = lax.dot_general(vcat, p.astype(BF16), _TN, preferred_element_type=F32)
        ot = ot[0:ATT_DH] / den
        for gi in range(group // 2):
            pair = jnp.concatenate([ot[:, (2 * gi) * BLK:(2 * gi + 1) * BLK],
                                    ot[:, (2 * gi + 1) * BLK:(2 * gi + 2) * BLK]], axis=0)
            cols = slice((kv * group // 2 + gi) * LANES, (kv * group // 2 + gi + 1) * LANES)
            o_ref[0, :, cols] = pair.T.astype(o_ref.dtype)


def _attn(aq, ak, av, ck, cv, sink_bc):
    B, T, _ = aq.shape
    L = ck.shape[1]
    nb = T // ATT_BLOCK
    kvw = ak.shape[2]
    cur = lambda b, n: (b, n, 0)
    prev = lambda b, n: (b, jnp.maximum(n - 1, 0), 0)
    nxt = lambda b, n: (b, jnp.minimum(n + 1, nb - 1), 0)
    kspec = lambda im: pl.BlockSpec((1, ATT_BLOCK, kvw), im)
    return pl.pallas_call(
        functools.partial(_attn_kernel, n_blocks=nb),
        grid=(B, nb),
        in_specs=[pl.BlockSpec((1, ATT_BLOCK, ATT_HEADS * ATT_DH), cur),
                  kspec(prev), kspec(cur), kspec(nxt), kspec(prev), kspec(cur), kspec(nxt),
                  pl.BlockSpec((1, L, kvw), lambda b, n: (b, 0, 0)),
                  pl.BlockSpec((1, L, kvw), lambda b, n: (b, 0, 0)),
                  pl.BlockSpec((ATT_HEADS, LANES), lambda b, n: (0, 0))],
        out_specs=pl.BlockSpec((1, ATT_BLOCK, ATT_HEADS * ATT_DH), cur),
        out_shape=jax.ShapeDtypeStruct((B, T, ATT_HEADS * ATT_DH), BF16),
        compiler_params=_params(("parallel", "parallel")),
        name="attn",
    )(aq, ak, ak, ak, av, av, av, ck, cv, sink_bc)


def _layernorm(v, g, b):
    mu = jnp.mean(v, axis=-1, keepdims=True)
    vc = v - mu
    var = jnp.mean(vc * vc, axis=-1, keepdims=True)
    return vc * lax.rsqrt(var + LN_EPS) * g + b


def _outproj_kernel(of_ref, ob_ref, r_ref, att_ref, x_ref, g1_ref, sh2_ref, sc2_ref,
                    wout_ref, gng_ref, lng_ref, lnb_ref, wqt_ref, sk_ref,
                    x1_ref, hpt_ref, st_ref, *, alpha):
    o = of_ref[0] + ob_ref[0]
    r = r_ref[0]
    gng = gng_ref[...]
    parts = []
    for h in range(GLA_HEADS):
        oh = o[:, h * GLA_DV:(h + 1) * GLA_DV]
        oh = oh * lax.rsqrt(jnp.mean(oh * oh, axis=-1, keepdims=True) + LN_EPS) * gng
        rh = r[:, h * GLA_DV:(h + 1) * GLA_DV]
        parts.append((oh * (rh * jax.nn.sigmoid(rh))).astype(BF16))
    parts.append(att_ref[0])
    cat = jnp.concatenate(parts, axis=1)
    y = jnp.dot(cat, wout_ref[...], preferred_element_type=F32)
    x1 = _layernorm(alpha * x_ref[0] + g1_ref[0] * y, lng_ref[...], lnb_ref[...])
    x1_ref[0] = x1
    hp = x1 * (1.0 + sc2_ref[0]) + sh2_ref[0]
    hpt = hp.T.astype(BF16)
    hpt_ref[0, 0] = hpt
    qt = jnp.dot(wqt_ref[...], hpt, preferred_element_type=F32)
    for h in range(PEER_HEADS):
        qh = qt[h * PEER_DQ:(h + 1) * PEER_DQ, :].astype(BF16)
        s = jnp.dot(sk_ref[h], qh, preferred_element_type=F32)
        st_ref[0, h] = s


def _outproj(of, ob, r, att, x, mod3, w_out, gng, ln_g, ln_b, wqt, skb, alpha, tm):
    B, T, D = x.shape
    W = of.shape[2]
    tok = lambda w: pl.BlockSpec((1, tm, w), lambda b, t: (b, t, 0))
    modrow = lambda j: pl.BlockSpec((1, 1, D), lambda b, t: (b * N_MOD + j, 0, 0))
    full = lambda a: pl.BlockSpec(a.shape, lambda b, t: (0,) * a.ndim)
    return pl.pallas_call(
        functools.partial(_outproj_kernel, alpha=alpha),
        grid=(B, T // tm),
        in_specs=[tok(W), tok(W), tok(W), tok(att.shape[2]), tok(D), modrow(2), modrow(3), modrow(4),
                  full(w_out), full(gng), full(ln_g), full(ln_b), full(wqt), full(skb)],
        out_specs=[tok(D),
                   pl.BlockSpec((1, 1, D, tm), lambda b, t: (b, t, 0, 0)),
                   pl.BlockSpec((1, PEER_HEADS, 2 * N_KEYS, tm), lambda b, t: (b, 0, 0, t))],
        out_shape=[jax.ShapeDtypeStruct((B, T, D), F32),
                   jax.ShapeDtypeStruct((B, T // tm, D, tm), BF16),
                   jax.ShapeDtypeStruct((B, PEER_HEADS, 2 * N_KEYS, T), F32)],
        compiler_params=_params(("parallel", "parallel")),
        name="outproj",
    )(of, ob, r, att, x, mod3, mod3, mod3, w_out, gng, ln_g, ln_b, wqt, skb)


def _sort_network(n):
    pairs = []

    def merge(lo, hi, r):
        step = 2 * r
        if step < hi - lo:
            merge(lo, hi, step)
            merge(lo + r, hi, step)
            pairs.extend((i, i + r) for i in range(lo + r, hi - r, step))
        else:
            pairs.append((lo, lo + r))

    def sort(lo, hi):
        if hi - lo >= 1:
            mid = lo + (hi - lo) // 2
            sort(lo, mid)
            sort(mid + 1, hi)
            merge(lo, hi, 1)

    sort(0, n - 1)
    return pairs


_SLABS = 16
_NET = _sort_network(_SLABS)


def _largest_desc(slabs, count):
    s = list(slabs) + [None] * (_SLABS - len(slabs))
    for i, j in _NET:
        if s[j] is None:
            continue
        if s[i] is None:
            s[i], s[j] = s[j], None
        else:
            s[i], s[j] = jnp.maximum(s[i], s[j]), jnp.minimum(s[i], s[j])
    depth = len(slabs)
    assert all(x is not None for x in s[:depth]) and all(x is None for x in s[depth:])
    vals = []
    for r in range(count):
        m = jnp.max(s[0], axis=0, keepdims=True)
        vals.append(m)
        hit = s[0] == m
        for l in range(min(depth, count - r - 1)):
            s[l] = jnp.where(hit, s[l + 1] if l + 1 < depth else NEG_INF, s[l])
    return vals


def _slabs(v):
    return [v[8 * k:8 * (k + 1)] for k in range(v.shape[0] // 8)]


def _topk_kernel(s_ref, g_ref, s2_ref, kap_ref):
    K = PEER_TOPK
    s1 = s_ref[0, 0, 0:N_KEYS, :]
    s2 = s_ref[0, 0, N_KEYS:2 * N_KEYS, :]
    a = _largest_desc(_slabs(s1), K + 1)
    b = _largest_desc(_slabs(s2), K + 1)
    bmat = jnp.concatenate(b[:K], axis=0)
    row8 = lax.broadcasted_iota(jnp.int32, (8, s1.shape[1]), 0)
    cands = _slabs(a[0] + bmat)
    for r in range(1, K // 2):
        cands.append(jnp.where(row8 < (K + 1) // (r + 1), a[r] + bmat[0:8], NEG_INF))
    cands.append(jnp.concatenate(a[K // 2:K], axis=0) + b[0])
    cands.append(jnp.where(row8 == 0, a[0] + b[K], jnp.where(row8 == 1, a[K] + b[0], NEG_INF)))
    top = _largest_desc(cands, K + 1)
    tau = 0.5 * (top[K - 1] + top[K])
    mx = top[0]
    zsum = top[0] * 0.0
    for r in range(K):
        zsum = zsum + jnp.exp(top[r] - mx)
    shift = mx + jnp.log(zsum)
    g = (s1 - shift) * LOG2E
    s2l = s2 * LOG2E
    kap = (tau - shift) * LOG2E
    for ci in range(s1.shape[1] // LANES):
        cols = slice(ci * LANES, (ci + 1) * LANES)
        g_ref[0, 0, ci] = g[:, cols]
        s2_ref[0, 0, ci] = s2l[:, cols]
        kap_ref[0, 0, ci] = kap[:, cols]


def _topk(st, tp):
    B, H, _, T = st.shape
    out = jax.ShapeDtypeStruct((B, H, T // LANES, N_KEYS, LANES), F32)
    ospec = pl.BlockSpec((1, 1, tp // LANES, N_KEYS, LANES), lambda b, h, t: (b, h, t, 0, 0))
    return pl.pallas_call(
        _topk_kernel,
        grid=(B, H, T // tp),
        in_specs=[pl.BlockSpec((1, 1, 2 * N_KEYS, tp), lambda b, h, t: (b, h, 0, t))],
        out_specs=[ospec, ospec,
                   pl.BlockSpec((1, 1, tp // LANES, 1, LANES), lambda b, h, t: (b, h, t, 0, 0))],
        out_shape=[out, out, jax.ShapeDtypeStruct((B, H, T // LANES, 1, LANES), F32)],
        compiler_params=_params(("parallel", "parallel", "parallel")),
        name="topk",
    )(st)


def _peer_kernel(hpt_ref, u_ref, un_ref, vt_ref, vp_ref, s2_ref, g_ref, kap_ref, x1_ref, g2_ref,
                 lng_ref, lnb_ref, o_ref, acc_ref, z0_ref, z1_ref, a0_ref, a1_ref, *, alpha, n_steps):
    e = pl.program_id(2)
    n_chunks = acc_ref.shape[0]
    te = u_ref.shape[0]
    n_i = te // N_KEYS
    sub = PEER_CHUNK // LANES
    upper = (e * n_i) % 8 >= n_i

    @pl.when(e == 0)
    def _():
        acc_ref[...] = jnp.zeros_like(acc_ref)
        a1_ref[...] = jnp.zeros_like(a1_ref)
        z0_ref[...] = jnp.dot(u_ref[...], hpt_ref[0, 0], preferred_element_type=F32)

    def chunk(k, z_cur, a_cur, z_next, a_prev):
        def next_scores():
            if k + 1 < n_chunks:
                z_next[...] = jnp.dot(u_ref[...], hpt_ref[0, k + 1], preferred_element_type=F32)
            else:
                z_next[...] = jnp.dot(un_ref[...], hpt_ref[0, 0], preferred_element_type=F32)

        def prev_output():
            if k > 0:
                acc_ref[k - 1] += jnp.dot(vt_ref[...], a_prev[...], preferred_element_type=F32)
            else:
                acc_ref[n_chunks - 1] += jnp.dot(vp_ref[...], a_prev[...], preferred_element_type=F32)

        def tile(ii, cc):
            rows = slice(ii * N_KEYS, (ii + 1) * N_KEYS)
            cols = slice(cc * LANES, (cc + 1) * LANES)
            c = k * sub + cc
            w = jnp.zeros((N_KEYS, LANES), F32)
            for h in range(PEER_HEADS):
                grow = g_ref[0, h, c, ii:ii + 1, :]
                if 2 * n_i <= 8:
                    grow = jnp.where(upper, g_ref[0, h, c, n_i + ii:n_i + ii + 1, :], grow)
                t = s2_ref[0, h, c] + grow
                w = w + jnp.exp2(jnp.where(t >= kap_ref[0, h, c], t, NEG_INF))
            zt = z_cur[rows, cols]
            act = zt * (1.0 + lax.erf(zt * (2.0 ** -0.5)))
            a_cur[rows, cols] = (w * act).astype(BF16)

        tiles = [(ii, cc) for ii in range(n_i) for cc in range(sub)]
        prev_output()
        for ii, cc in tiles[:len(tiles) // 2]:
            tile(ii, cc)
        next_scores()
        for ii, cc in tiles[len(tiles) // 2:]:
            tile(ii, cc)

    for k in range(n_chunks):
        if k % 2 == 0:
            chunk(k, z0_ref, a0_ref, z1_ref, a1_ref)
        else:
            chunk(k, z1_ref, a1_ref, z0_ref, a0_ref)

    @pl.when(e == n_steps - 1)
    def _():
        acc_ref[n_chunks - 1] += jnp.dot(vt_ref[...], a1_ref[...], preferred_element_type=F32)
        for k in range(n_chunks):
            rows = slice(k * PEER_CHUNK, (k + 1) * PEER_CHUNK)
            y = acc_ref[k].T
            o_ref[0, rows, :] = _layernorm(alpha * x1_ref[0, rows, :] + g2_ref[0] * y,
                                           lng_ref[...], lnb_ref[...])


def _peer(hpt, u, vt, s2, g, kap, x1, mod3, ln_g, ln_b, alpha, tm, te):
    B, _, D, _ = hpt.shape
    T = x1.shape[1]
    E = u.shape[0]
    n_steps = E // te
    nc = tm // LANES
    nk = tm // PEER_CHUNK
    side = pl.BlockSpec((1, PEER_HEADS, nc, N_KEYS, LANES), lambda b, t, e: (b, 0, t, 0, 0))
    n_i = te // N_KEYS
    assert n_i in (4, 8) and nk % 2 == 0
    keyrows = pl.BlockSpec((1, PEER_HEADS, nc, 8, LANES), lambda b, t, e: (b, 0, t, (e * n_i) // 8, 0))
    return pl.pallas_call(
        functools.partial(_peer_kernel, alpha=alpha, n_steps=n_steps),
        grid=(B, T // tm, n_steps),
        in_specs=[pl.BlockSpec((1, nk, D, PEER_CHUNK), lambda b, t, e: (b, t, 0, 0)),
                  pl.BlockSpec((te, D), lambda b, t, e: (e, 0)),
                  pl.BlockSpec((te, D), lambda b, t, e: (jnp.minimum(e + 1, n_steps - 1), 0)),
                  pl.BlockSpec((D, te), lambda b, t, e: (0, e)),
                  pl.BlockSpec((D, te), lambda b, t, e: (0, jnp.maximum(e - 1, 0))),
                  side, keyrows,
                  pl.BlockSpec((1, PEER_HEADS, nc, 1, LANES), lambda b, t, e: (b, 0, t, 0, 0)),
                  pl.BlockSpec((1, tm, D), lambda b, t, e: (b, t, 0), pipeline_mode=pl.Buffered(1)),
                  pl.BlockSpec((1, 1, D), lambda b, t, e: (b * N_MOD + 5, 0, 0)),
                  pl.BlockSpec((1, D), lambda b, t, e: (0, 0)),
                  pl.BlockSpec((1, D), lambda b, t, e: (0, 0))],
        out_specs=pl.BlockSpec((1, tm, D), lambda b, t, e: (b, t, 0)),
        out_shape=jax.ShapeDtypeStruct((B, T, D), F32),
        scratch_shapes=[pltpu.VMEM((nk, D, PEER_CHUNK), F32),
                        pltpu.VMEM((te, PEER_CHUNK), F32), pltpu.VMEM((te, PEER_CHUNK), F32),
                        pltpu.VMEM((te, PEER_CHUNK), BF16), pltpu.VMEM((te, PEER_CHUNK), BF16)],
        compiler_params=_params(("parallel", "parallel", "arbitrary")),
        name="peer",
    )(hpt, u, u, vt, vt, s2, g, kap, x1, mod3, ln_g, ln_b)


def _rope_tables(n_tokens):
    rows = n_tokens // GRID_W
    row = jnp.repeat(jnp.arange(rows, dtype=F32), GRID_W)
    col = jnp.tile(jnp.arange(GRID_W, dtype=F32), rows)
    n_freq = ATT_DH // 4
    inv = ROPE_THETA ** (-jnp.arange(n_freq, dtype=F32) / n_freq)
    ar, ac = row[:, None] * inv, col[:, None] * inv
    cos = jnp.concatenate([jnp.cos(ar), jnp.cos(ar), jnp.cos(ac), jnp.cos(ac)], axis=1)
    sin = jnp.concatenate([-jnp.sin(ar), jnp.sin(ar), -jnp.sin(ac), jnp.sin(ac)], axis=1)
    return jnp.tile(cos, (1, 2)), jnp.tile(sin, (1, 2))


def _dup_heads(w, n_heads, dh):
    return jnp.concatenate([w[:, h * dh:(h + 1) * dh] for h in range(n_heads) for _ in range(2)], axis=1)


def kernel(x, c, ctx, c_ctx, w_ada, b_ada, w_in, w_gate2_f, b_gate_f, w_gate2_b, b_gate_b, gla_norm_g,
           attn_sink, w_out, ln1_g, ln1_b, peer_wq, peer_subkeys, peer_u, peer_v, ln2_g, ln2_b):
    B, T, D = x.shape
    L = ctx.shape[1]
    depth = w_ada.shape[0]
    assert depth == 1, "single-layer block only"
    alpha = (2.0 * depth) ** 0.25

    cond = jnp.concatenate([jax.nn.silu(c), jax.nn.silu(c_ctx)[None]], axis=0)
    cond_bc = jnp.broadcast_to(cond[:, :, None], (B + 1, D, LANES))
    mod = _ada(cond_bc, w_ada[0], b_ada[0][None])
    mod3 = mod.reshape(8 * N_MOD, 1, D)

    wi = w_in[0]
    kq = 2 * GLA_HEADS * GLA_DK + 2 * GLA_HEADS * GLA_DV
    z0, a0 = kq, kq + 2 * GLA_RANK
    k0 = a0 + ATT_HEADS * ATT_DH
    v0 = k0 + ATT_KV_HEADS * ATT_DH
    w_p = jnp.concatenate([
        wi[:, :kq], wi[:, a0:k0],
        _dup_heads(wi[:, k0:v0], ATT_KV_HEADS, ATT_DH),
        _dup_heads(wi[:, v0:], ATT_KV_HEADS, ATT_DH),
        wi[:, z0:a0], jnp.zeros((D, LANES - 2 * GLA_RANK), F32)], axis=1).astype(BF16)
    gw = GLA_HEADS * GLA_DK
    w2 = jnp.zeros((LANES, 2 * gw), F32)
    w2 = w2.at[0:GLA_RANK, 0:gw].set(w_gate2_f[0]).at[GLA_RANK:2 * GLA_RANK, gw:].set(w_gate2_b[0]).astype(BF16)
    bg = jnp.concatenate([b_gate_f[0], b_gate_b[0]])[None]
    cos, sin = _rope_tables(T)
    ones = jnp.ones((L, LANES), F32)

    gq, gk, gv, r, g, aq, ak, av = _inproj(x, mod3, lambda b: b * N_MOD + 1, lambda b: b * N_MOD,
                                           w_p, w2, bg, cos, sin, tm=min(T, 512))
    _, cgk, cgv, _, cg, _, cak, cav = _inproj(ctx, mod3, lambda b: B * N_MOD + 1, lambda b: B * N_MOD,
                                              w_p, w2, bg, ones, 0.0 * ones, tm=L)

    of, ob = _gla(gq, gk, g, gv, cgk, cg, cgv, tt=min(T, 512))
    sink_bc = jnp.broadcast_to(attn_sink[0][:, None], (ATT_HEADS, LANES))
    att = _attn(aq, ak, av, cak, cav, sink_bc)

    sk = peer_subkeys[0]
    half = PEER_DQ // 2
    skb = jnp.zeros((PEER_HEADS, 2 * N_KEYS, PEER_DQ), F32)
    skb = skb.at[:, :N_KEYS, :half].set(sk[:, 0]).at[:, N_KEYS:, half:].set(sk[:, 1])
    skb = skb.astype(BF16)
    x1, hpt, st = _outproj(of, ob, r, att, x, mod3, w_out[0].astype(BF16), gla_norm_g[0][None],
                           ln1_g[0][None], ln1_b[0][None], peer_wq[0].T.astype(BF16), skb,
                           alpha, tm=PEER_CHUNK)

    g, s2, kap = _topk(st, tp=min(T, 512))
    u = peer_u[0].astype(BF16)
    vt = (0.5 * peer_v[0]).T.astype(BF16)
    return _peer(hpt, u, vt, s2, g, kap, x1, mod3, ln2_g[0][None], ln2_b[0][None],
                 alpha, tm=min(T, 1024), te=1024)
```

```python
import functools

import jax
import jax.numpy as jnp
from jax import lax
from jax.experimental import pallas as pl
from jax.experimental.pallas import tpu as pltpu

F32 = jnp.float32
BF16 = jnp.bfloat16

GRID_W = 64
GLA_HEADS = 4
GLA_DK = 64
GLA_DV = 128
GLA_RANK = 16
GLA_NORMALIZER = 16.0
GLA_CHUNK = 64
ATT_HEADS = 8
ATT_KV_HEADS = 2
ATT_DH = 64
WINDOW = 128
ATT_BLOCK = 128
ROPE_THETA = 10000.0
PEER_HEADS = 8
N_KEYS = 128
PEER_DQ = 128
PEER_TOPK = 16
N_MOD = 6
LN_EPS = 1e-5
NEG_INF = -1e30
LANES = 128
PEER_CHUNK = 256
LOG2E = 1.4426950408889634
MXU_TILE = 256

VMEM_LIMIT = 56 * 1024 * 1024
PEER_VMEM_LIMIT = 60 * 1024 * 1024

_NT = (((1,), (1,)), ((), ()))
_TN = (((0,), (0,)), ((), ()))


def _params(sem, vmem_limit=VMEM_LIMIT):
    return pltpu.CompilerParams(dimension_semantics=sem, vmem_limit_bytes=vmem_limit)


def _ada_kernel(s_ref, w_ref, b_ref, o_ref, *, n_rows):
    tn = w_ref.shape[1]
    w = w_ref[...]
    rows = []
    for r in range(n_rows):
        s = s_ref[r]
        cols = [jnp.sum(w[:, j * LANES:(j + 1) * LANES] * s, axis=0, keepdims=True)
                for j in range(tn // LANES)]
        rows.append(jnp.concatenate(cols, axis=1) + b_ref[...])
    rows.append(jnp.zeros((8 - n_rows, tn), F32))
    o_ref[...] = jnp.concatenate(rows, axis=0)


def _ada(cond_bc, w, b):
    n_rows, d, _ = cond_bc.shape
    n = w.shape[1]
    tn = 512
    return pl.pallas_call(
        functools.partial(_ada_kernel, n_rows=n_rows),
        grid=(n // tn,),
        in_specs=[pl.BlockSpec((n_rows, d, LANES), lambda j: (0, 0, 0)),
                  pl.BlockSpec((d, tn), lambda j: (0, j)),
                  pl.BlockSpec((1, tn), lambda j: (0, j))],
        out_specs=pl.BlockSpec((8, tn), lambda j: (0, j)),
        out_shape=jax.ShapeDtypeStruct((8, n), F32),
        compiler_params=_params(("parallel",)),
        name="ada",
    )(cond_bc, w, b)


def _rope(a, cos, sin):
    n = a.shape[1] // LANES
    c = jnp.concatenate([cos] * n, axis=1)
    s = jnp.concatenate([sin] * n, axis=1)
    lane = lax.broadcasted_iota(jnp.int32, a.shape, 1)
    swapped = jnp.where((lane % 32) < 16,
                        pltpu.roll(a, a.shape[1] - 16, 1),
                        pltpu.roll(a, 16, 1))
    return a * c + swapped * s


def _inproj_kernel(x_ref, sc_ref, sh_ref, w_ref, w2_ref, bg_ref, cos_ref, sin_ref,
                   gq_ref, gk_ref, gv_ref, r_ref, g_ref, aq_ref, ak_ref, av_ref):
    h = x_ref[0] * (1.0 + sc_ref[0]) + sh_ref[0]
    y = jnp.dot(h.astype(BF16), w_ref[...], preferred_element_type=F32)
    gq_ref[0] = y[:, 0:256] * (GLA_DK ** -0.5)
    gk_ref[0] = y[:, 256:512]
    gv_ref[0] = y[:, 512:1024].astype(BF16)
    r_ref[0] = y[:, 1024:1536]
    cos = cos_ref[...]
    sin = sin_ref[...]
    aq_ref[0] = (_rope(y[:, 1536:2048], cos, sin) * (ATT_DH ** -0.5)).astype(BF16)
    ak_ref[0] = _rope(y[:, 2048:2304], cos, sin).astype(BF16)
    av_ref[0] = y[:, 2304:2560].astype(BF16)
    z = y[:, 2560:2688]
    logits = jnp.dot(z.astype(BF16), w2_ref[...], preferred_element_type=F32) + bg_ref[...]
    g = (jnp.minimum(logits, 0.0) - jnp.log1p(jnp.exp(-jnp.abs(logits)))) * (1.0 / GLA_NORMALIZER)
    g_ref[0, 0] = g[:, 0:256]
    g_ref[1, 0] = g[:, 256:512]


def _inproj(x, mod3, sc_row, sh_row, w_p, w2, bg, cos, sin, tm):
    B, T, D = x.shape
    NP = w_p.shape[1]
    tok = lambda w, dt: jax.ShapeDtypeStruct((B, T, w), dt)
    tspec = lambda w: pl.BlockSpec((1, tm, w), lambda b, t: (b, t, 0))
    return pl.pallas_call(
        _inproj_kernel,
        grid=(B, T // tm),
        in_specs=[tspec(D),
                  pl.BlockSpec((1, 1, D), lambda b, t: (sc_row(b), 0, 0)),
                  pl.BlockSpec((1, 1, D), lambda b, t: (sh_row(b), 0, 0)),
                  pl.BlockSpec((D, NP), lambda b, t: (0, 0)),
                  pl.BlockSpec((LANES, 512), lambda b, t: (0, 0)),
                  pl.BlockSpec((1, 512), lambda b, t: (0, 0)),
                  pl.BlockSpec((tm, LANES), lambda b, t: (t, 0)),
                  pl.BlockSpec((tm, LANES), lambda b, t: (t, 0))],
        out_specs=[tspec(256), tspec(256), tspec(512), tspec(512),
                   pl.BlockSpec((2, 1, tm, 256), lambda b, t: (0, b, t, 0)),
                   tspec(512), tspec(256), tspec(256)],
        out_shape=[tok(256, F32), tok(256, F32), tok(512, BF16), tok(512, F32),
                   jax.ShapeDtypeStruct((2, B, T, 256), F32),
                   tok(512, BF16), tok(256, BF16), tok(256, BF16)],
        compiler_params=_params(("parallel", "parallel")),
        name="inproj",
    )(x, mod3, mod3, w_p, w2, bg, cos, sin)


def _gla_kernel(qf_ref, kf_ref, gf_ref, vf_ref, qb_ref, kb_ref, gb_ref, vb_ref, ck_ref, cg_ref, cv_ref,
                of_ref, ob_ref, st_ref, *, n_chunks, n_ctx_chunks):
    t = pl.program_id(0)
    nb = qf_ref.shape[0]
    C = GLA_CHUNK
    npairs = GLA_HEADS // 2
    ri = lax.broadcasted_iota(jnp.int32, (C, C), 0)
    ci = lax.broadcasted_iota(jnp.int32, (C, C), 1)
    tri = [ri >= ci, ri <= ci]
    tri_bf = [m.astype(BF16) for m in tri]
    tri2 = [jnp.concatenate([m, m], axis=0) for m in tri]
    lane = lax.broadcasted_iota(jnp.int32, (1, LANES), 1)
    m0 = (lane < GLA_DK).astype(F32)
    m1 = 1.0 - m0
    rr = lax.broadcasted_iota(jnp.int32, (2 * GLA_DV, LANES), 0)
    ll = lax.broadcasted_iota(jnp.int32, (2 * GLA_DV, LANES), 1)
    blockdiag = (rr < GLA_DV) == (ll < GLA_DK)

    def chunks(items, with_out):
        bs = []
        for (_, d, _), _, gc, _, _ in items:
            ghi = gc.astype(BF16)
            glo = (gc - ghi.astype(F32)).astype(BF16)
            bs.append(jnp.dot(tri_bf[d], ghi, preferred_element_type=F32)
                      + jnp.dot(tri_bf[d], glo, preferred_element_type=F32))
        btots = [jnp.sum(it[2], axis=0, keepdims=True) for it in items]
        sts = [st_ref[it[0]] for it in items]
        outs = [None] * len(items)
        if with_out:
            qes = [it[4] * jnp.exp(b) for it, b in zip(items, bs)]
            amats = []
            for it, b, qe in zip(items, bs, qes):
                ke = (it[1] * jnp.exp(-b)).astype(BF16)
                lhs = jnp.concatenate([qe * m0, qe * m1], axis=0).astype(BF16)
                amats.append(lax.dot_general(lhs, ke, _NT, preferred_element_type=F32))
            inters = [lax.dot_general(qe.astype(BF16), st.astype(BF16), _NT, preferred_element_type=F32)
                      for qe, st in zip(qes, sts)]
            for n, (it, a) in enumerate(zip(items, amats)):
                a = jnp.where(tri2[it[0][1]], a, 0.0).astype(BF16)
                oi = jnp.dot(a, it[3], preferred_element_type=F32)
                outs[n] = jnp.concatenate([oi[:C, :GLA_DV], oi[C:, GLA_DV:]], axis=1) + inters[n]
        for it, b, btot, st in zip(items, bs, btots, sts):
            kd = (it[1] * jnp.exp(btot - b)).astype(BF16)
            mt = lax.dot_general(it[3], kd, _TN, preferred_element_type=F32)
            st_ref[it[0]] = st * jnp.exp(btot) + jnp.where(blockdiag, mt, 0.0)
        return outs

    def qk_lanes(p):
        return slice(p * LANES, (p + 1) * LANES)

    def v_lanes(p):
        return slice(p * 2 * GLA_DV, (p + 1) * 2 * GLA_DV)

    @pl.when(t == 0)
    def _():
        st_ref[...] = jnp.zeros_like(st_ref)
        for j in range(n_ctx_chunks):
            items = []
            for bi in range(nb):
                for d in range(2):
                    cj = j if d == 0 else n_ctx_chunks - 1 - j
                    rows = slice(cj * C, (cj + 1) * C)
                    items += [((bi, d, p), ck_ref[bi, rows, qk_lanes(p)], cg_ref[d, bi, rows, qk_lanes(p)],
                               cv_ref[bi, rows, v_lanes(p)], None) for p in range(npairs)]
            chunks(items, False)

    streams = [(qf_ref, kf_ref, gf_ref, vf_ref, of_ref), (qb_ref, kb_ref, gb_ref, vb_ref, ob_ref)]

    def body(j, carry):
        items, dests = [], []
        for bi in range(nb):
            for d, (q_ref, k_ref, g_ref, v_ref, o_ref) in enumerate(streams):
                cj = j if d == 0 else n_chunks - 1 - j
                rows = pl.ds(pl.multiple_of(cj * C, C), C)
                for p in range(npairs):
                    items.append(((bi, d, p), k_ref[bi, rows, qk_lanes(p)], g_ref[0, bi, rows, qk_lanes(p)],
                                  v_ref[bi, rows, v_lanes(p)], q_ref[bi, rows, qk_lanes(p)]))
                    dests.append((o_ref, bi, rows, v_lanes(p)))
        for (o_ref, bi, rows, lanes), out in zip(dests, chunks(items, True)):
            o_ref[bi, rows, lanes] = out
        return carry

    lax.fori_loop(0, n_chunks, body, 0)


def _gla(gq, gk, g, gv, ck, cg, cv, tt):
    B, T, W = gq.shape
    L = ck.shape[1]
    nT = T // tt
    VW = gv.shape[2]
    fwd = lambda t: (0, t, 0)
    bwd = lambda t: (0, nT - 1 - t, 0)
    ctx = lambda t: (0, 0, 0)
    out = jax.ShapeDtypeStruct((B, T, VW), F32)
    return pl.pallas_call(
        functools.partial(_gla_kernel, n_chunks=tt // GLA_CHUNK, n_ctx_chunks=L // GLA_CHUNK),
        grid=(nT,),
        in_specs=[pl.BlockSpec((B, tt, W), fwd), pl.BlockSpec((B, tt, W), fwd),
                  pl.BlockSpec((1, B, tt, W), lambda t: (0, 0, t, 0)),
                  pl.BlockSpec((B, tt, VW), fwd),
                  pl.BlockSpec((B, tt, W), bwd), pl.BlockSpec((B, tt, W), bwd),
                  pl.BlockSpec((1, B, tt, W), lambda t: (1, 0, nT - 1 - t, 0)),
                  pl.BlockSpec((B, tt, VW), bwd),
                  pl.BlockSpec((B, L, W), ctx),
                  pl.BlockSpec((2, B, L, W), lambda t: (0, 0, 0, 0)),
                  pl.BlockSpec((B, L, VW), ctx)],
        out_specs=[pl.BlockSpec((B, tt, VW), fwd), pl.BlockSpec((B, tt, VW), bwd)],
        out_shape=[out, out],
        scratch_shapes=[pltpu.VMEM((B, 2, GLA_HEADS // 2, 2 * GLA_DV, LANES), F32)],
        compiler_params=_params(("arbitrary",)),
        name="gla",
    )(gq, gk, g, gv, gq, gk, g, gv, ck, cg, cv)


def _attn_kernel(q_ref, kp_ref, kc_ref, kn_ref, vp_ref, vc_ref, vn_ref, ck_ref, cv_ref, sink_ref,
                 o_ref, *, n_blocks):
    n = pl.program_id(1)
    BLK = ATT_BLOCK
    L = ck_ref.shape[1]
    nk = 3 * BLK + L
    group = ATT_HEADS // ATT_KV_HEADS
    ki = lax.broadcasted_iota(jnp.int32, (nk, group * BLK), 0)
    qi = lax.broadcasted_iota(jnp.int32, (nk, group * BLK), 1) & (BLK - 1)
    band = jnp.abs(ki - BLK - qi) <= WINDOW
    prev_ok = jnp.logical_or(ki >= BLK, n > 0)
    next_ok = jnp.logical_or(ki < 2 * BLK, n < n_blocks - 1)
    valid = jnp.logical_or(ki >= 3 * BLK, band & prev_ok & next_ok)
    lane = lax.broadcasted_iota(jnp.int32, (BLK, LANES), 1)
    zero = jnp.zeros((), BF16)
    for kv in range(ATT_KV_HEADS):
        ks = slice(kv * LANES, (kv + 1) * LANES)
        kcat = jnp.concatenate([kp_ref[0, :, ks], kc_ref[0, :, ks], kn_ref[0, :, ks], ck_ref[0, :, ks]], axis=0)
        vcat = jnp.concatenate([vp_ref[0, :, ks], vc_ref[0, :, ks], vn_ref[0, :, ks], cv_ref[0, :, ks]], axis=0)
        qparts = []
        for gi in range(group // 2):
            q2 = q_ref[0, :, (kv * group // 2 + gi) * LANES:(kv * group // 2 + gi + 1) * LANES]
            qparts += [jnp.where(lane < ATT_DH, q2, zero), jnp.where(lane >= ATT_DH, q2, zero)]
        qstack = jnp.concatenate(qparts, axis=0)
        s = lax.dot_general(kcat, qstack, _NT, preferred_element_type=F32)
        s = jnp.where(valid, s, NEG_INF)
        snk = jnp.concatenate([sink_ref[kv * group + hh:kv * group + hh + 1, :] for hh in range(group)], axis=1)
        m = jnp.maximum(jnp.max(s, axis=0, keepdims=True), snk)
        p = jnp.exp(s - m)
        den = jnp.sum(p, axis=0, keepdims=True) + jnp.exp(snk - m)
        ot = lax.dot_general(vcat, p.astype(BF16), _TN, preferred_element_type=F32)
        ot = ot[0:ATT_DH] / den
        for gi in range(group // 2):
            pair = jnp.concatenate([ot[:, (2 * gi) * BLK:(2 * gi + 1) * BLK],
                                    ot[:, (2 * gi + 1) * BLK:(2 * gi + 2) * BLK]], axis=0)
            cols = slice((kv * group // 2 + gi) * LANES, (kv * group // 2 + gi + 1) * LANES)
            o_ref[0, :, cols] = pair.T.astype(o_ref.dtype)


def _attn(aq, ak, av, ck, cv, sink_bc):
    B, T, _ = aq.shape
    L = ck.shape[1]
    nb = T // ATT_BLOCK
    kvw = ak.shape[2]
    cur = lambda b, n: (b, n, 0)
    prev = lambda b, n: (b, jnp.maximum(n - 1, 0), 0)
    nxt = lambda b, n: (b, jnp.minimum(n + 1, nb - 1), 0)
    kspec = lambda im: pl.BlockSpec((1, ATT_BLOCK, kvw), im)
    return pl.pallas_call(
        functools.partial(_attn_kernel, n_blocks=nb),
        grid=(B, nb),
        in_specs=[pl.BlockSpec((1, ATT_BLOCK, ATT_HEADS * ATT_DH), cur),
                  kspec(prev), kspec(cur), kspec(nxt), kspec(prev), kspec(cur), kspec(nxt),
                  pl.BlockSpec((1, L, kvw), lambda b, n: (b, 0, 0)),
                  pl.BlockSpec((1, L, kvw), lambda b, n: (b, 0, 0)),
                  pl.BlockSpec((ATT_HEADS, LANES), lambda b, n: (0, 0))],
        out_specs=pl.BlockSpec((1, ATT_BLOCK, ATT_HEADS * ATT_DH), cur),
        out_shape=jax.ShapeDtypeStruct((B, T, ATT_HEADS * ATT_DH), BF16),
        compiler_params=_params(("parallel", "parallel")),
        name="attn",
    )(aq, ak, ak, ak, av, av, av, ck, cv, sink_bc)


def _layernorm(v, g, b):
    mu = jnp.mean(v, axis=-1, keepdims=True)
    vc = v - mu
    var = jnp.mean(vc * vc, axis=-1, keepdims=True)
    return vc * lax.rsqrt(var + LN_EPS) * g + b


def _outproj_kernel(of_ref, ob_ref, r_ref, att_ref, x_ref, g1_ref, sh2_ref, sc2_ref,
                    wout_ref, gng_ref, lng_ref, lnb_ref, wqt_ref, sk_ref,
                    x1_ref, hpt_ref, st_ref, *, alpha):
    o = of_ref[0] + ob_ref[0]
    r = r_ref[0]
    gng = gng_ref[...]
    parts = []
    for h in range(GLA_HEADS):
        oh = o[:, h * GLA_DV:(h + 1) * GLA_DV]
        oh = oh * lax.rsqrt(jnp.mean(oh * oh, axis=-1, keepdims=True) + LN_EPS) * gng
        rh = r[:, h * GLA_DV:(h + 1) * GLA_DV]
        parts.append((oh * (rh * jax.nn.sigmoid(rh))).astype(BF16))
    parts.append(att_ref[0])
    cat = jnp.concatenate(parts, axis=1)
    y = jnp.dot(cat, wout_ref[...], preferred_element_type=F32)
    x1 = _layernorm(alpha * x_ref[0] + g1_ref[0] * y, lng_ref[...], lnb_ref[...])
    x1_ref[0] = x1
    hp = x1 * (1.0 + sc2_ref[0]) + sh2_ref[0]
    hpt = hp.T.astype(BF16)
    for c in range(hpt.shape[1] // PEER_CHUNK):
        hpt_ref[0, c] = hpt[:, c * PEER_CHUNK:(c + 1) * PEER_CHUNK]
    qt = jnp.dot(wqt_ref[...], hpt, preferred_element_type=F32)
    for h in range(PEER_HEADS):
        qh = qt[h * PEER_DQ:(h + 1) * PEER_DQ, :].astype(BF16)
        s = jnp.dot(sk_ref[h], qh, preferred_element_type=F32) * LOG2E
        for c in range(s.shape[1] // LANES):
            st_ref[0, h, c] = s[:, c * LANES:(c + 1) * LANES]


def _outproj(of, ob, r, att, x, mod3, w_out, gng, ln_g, ln_b, wqt, skb, alpha, tm):
    B, T, D = x.shape
    W = of.shape[2]
    tok = lambda w: pl.BlockSpec((1, tm, w), lambda b, t: (b, t, 0))
    modrow = lambda j: pl.BlockSpec((1, 1, D), lambda b, t: (b * N_MOD + j, 0, 0))
    full = lambda a: pl.BlockSpec(a.shape, lambda b, t: (0,) * a.ndim)
    return pl.pallas_call(
        functools.partial(_outproj_kernel, alpha=alpha),
        grid=(B, T // tm),
        in_specs=[tok(W), tok(W), tok(W), tok(att.shape[2]), tok(D), modrow(2), modrow(3), modrow(4),
                  full(w_out), full(gng), full(ln_g), full(ln_b), full(wqt), full(skb)],
        out_specs=[tok(D),
                   pl.BlockSpec((1, tm // PEER_CHUNK, D, PEER_CHUNK), lambda b, t: (b, t, 0, 0)),
                   pl.BlockSpec((1, PEER_HEADS, tm // LANES, 2 * N_KEYS, LANES), lambda b, t: (b, 0, t, 0, 0))],
        out_shape=[jax.ShapeDtypeStruct((B, T, D), F32),
                   jax.ShapeDtypeStruct((B, T // PEER_CHUNK, D, PEER_CHUNK), BF16),
                   jax.ShapeDtypeStruct((B, PEER_HEADS, T // LANES, 2 * N_KEYS, LANES), F32)],
        compiler_params=_params(("parallel", "parallel")),
        name="outproj",
    )(of, ob, r, att, x, mod3, mod3, mod3, w_out, gng, ln_g, ln_b, wqt, skb)


def _sort_network(n):
    pairs = []

    def merge(lo, hi, r):
        step = 2 * r
        if step < hi - lo:
            merge(lo, hi, step)
            merge(lo + r, hi, step)
            pairs.extend((i, i + r) for i in range(lo + r, hi - r, step))
        else:
            pairs.append((lo, lo + r))

    def sort(lo, hi):
        if hi - lo >= 1:
            mid = lo + (hi - lo) // 2
            sort(lo, mid)
            sort(mid + 1, hi)
            merge(lo, hi, 1)

    sort(0, n - 1)
    return pairs


_SLABS = 16
_NET = _sort_network(_SLABS)


def _largest_desc(slabs, count):
    s = list(slabs) + [None] * (_SLABS - len(slabs))
    for i, j in _NET:
        if s[j] is None:
            continue
        if s[i] is None:
            s[i], s[j] = s[j], None
        else:
            s[i], s[j] = jnp.maximum(s[i], s[j]), jnp.minimum(s[i], s[j])
    depth = len(slabs)
    assert all(x is not None for x in s[:depth]) and all(x is None for x in s[depth:])
    vals = []
    for r in range(count):
        m = jnp.max(s[0], axis=0, keepdims=True)
        vals.append(m)
        hit = s[0] == m
        for l in range(min(depth, count - r - 1)):
            s[l] = jnp.where(hit, s[l + 1] if l + 1 < depth else NEG_INF, s[l])
    return vals


def _slabs(v):
    return [v[8 * k:8 * (k + 1)] for k in range(v.shape[0] // 8)]


def _topk_kernel(s_ref, g_ref, kap_ref):
    K = PEER_TOPK
    s = jnp.concatenate([s_ref[0, 0, c] for c in range(s_ref.shape[2])], axis=1)
    s1 = s[0:N_KEYS]
    s2 = s[N_KEYS:2 * N_KEYS]
    a = _largest_desc(_slabs(s1), K + 1)
    b = _largest_desc(_slabs(s2), K + 1)
    bmat = jnp.concatenate(b[:K], axis=0)
    row8 = lax.broadcasted_iota(jnp.int32, (8, s1.shape[1]), 0)
    cands = _slabs(a[0] + bmat)
    for r in range(1, K // 2):
        cands.append(jnp.where(row8 < (K + 1) // (r + 1), a[r] + bmat[0:8], NEG_INF))
    cands.append(jnp.concatenate(a[K // 2:K], axis=0) + b[0])
    cands.append(jnp.where(row8 == 0, a[0] + b[K], jnp.where(row8 == 1, a[K] + b[0], NEG_INF)))
    top = _largest_desc(cands, K + 1)
    tau = 0.5 * (top[K - 1] + top[K])
    mx = top[0]
    zsum = top[0] * 0.0
    for r in range(K):
        zsum = zsum + jnp.exp2(top[r] - mx)
    shift = mx + jnp.log2(zsum)
    g = s1 - shift
    kap = tau - shift
    for ci in range(s1.shape[1] // LANES):
        cols = slice(ci * LANES, (ci + 1) * LANES)
        g_ref[0, 0, ci] = g[:, cols]
        kap_ref[0, 0, ci] = kap[:, cols]


def _topk(st, tp):
    B, H, nchunks, _, _ = st.shape
    out = jax.ShapeDtypeStruct((B, H, nchunks, N_KEYS, LANES), F32)
    ospec = pl.BlockSpec((1, 1, tp // LANES, N_KEYS, LANES), lambda b, h, t: (b, h, t, 0, 0))
    return pl.pallas_call(
        _topk_kernel,
        grid=(B, H, nchunks * LANES // tp),
        in_specs=[pl.BlockSpec((1, 1, tp // LANES, 2 * N_KEYS, LANES), lambda b, h, t: (b, h, t, 0, 0))],
        out_specs=[ospec,
                   pl.BlockSpec((1, 1, tp // LANES, 1, LANES), lambda b, h, t: (b, h, t, 0, 0))],
        out_shape=[out, jax.ShapeDtypeStruct((B, H, nchunks, 1, LANES), F32)],
        compiler_params=_params(("parallel", "parallel", "parallel")),
        name="topk",
    )(st)


def _peer_kernel(hpt_ref, u_ref, un_ref, vt_ref, vp_ref, s2_ref, g_ref, kap_ref, x1_ref, g2_ref,
                 lng_ref, lnb_ref, o_ref, acc_ref, z0_ref, z1_ref, a0_ref, a1_ref, *, alpha, n_steps):
    e = pl.program_id(2)
    n_chunks = acc_ref.shape[0]
    te = u_ref.shape[0]
    n_i = te // N_KEYS
    sub = PEER_CHUNK // LANES
    upper = (e * n_i) % 8 >= n_i

    @pl.when(e == 0)
    def _():
        acc_ref[...] = jnp.zeros_like(acc_ref)
        a1_ref[...] = jnp.zeros_like(a1_ref)
        z0_ref[...] = jnp.dot(u_ref[...], hpt_ref[0, 0], preferred_element_type=F32)

    def chunk(k, z_cur, a_cur, z_next, a_prev):
        def next_scores():
            if k + 1 < n_chunks:
                z_next[...] = jnp.dot(u_ref[...], hpt_ref[0, k + 1], preferred_element_type=F32)
            else:
                z_next[...] = jnp.dot(un_ref[...], hpt_ref[0, 0], preferred_element_type=F32)

        def prev_output():
            if k > 0:
                acc_ref[k - 1] += jnp.dot(vt_ref[...], a_prev[...], preferred_element_type=F32)
            else:
                acc_ref[n_chunks - 1] += jnp.dot(vp_ref[...], a_prev[...], preferred_element_type=F32)

        def tile(ii, cc):
            rows = slice(ii * N_KEYS, (ii + 1) * N_KEYS)
            cols = slice(cc * LANES, (cc + 1) * LANES)
            c = k * sub + cc
            w = jnp.zeros((N_KEYS, LANES), F32)
            for h in range(PEER_HEADS):
                grow = g_ref[0, h, c, ii:ii + 1, :]
                if 2 * n_i <= 8:
                    grow = jnp.where(upper, g_ref[0, h, c, n_i + ii:n_i + ii + 1, :], grow)
                t = s2_ref[0, h, c] + grow
                w = w + jnp.exp2(jnp.where(t >= kap_ref[0, h, c], t, NEG_INF))
            zt = z_cur[rows, cols]
            act = zt * (1.0 + lax.erf(zt * (2.0 ** -0.5)))
            a_cur[rows, cols] = (w * act).astype(BF16)

        tiles = [(ii, cc) for ii in range(n_i) for cc in range(sub)]
        prev_output()
        for ii, cc in tiles[:len(tiles) // 2]:
            tile(ii, cc)
        next_scores()
        for ii, cc in tiles[len(tiles) // 2:]:
            tile(ii, cc)

    for k in range(n_chunks):
        if k % 2 == 0:
            chunk(k, z0_ref, a0_ref, z1_ref, a1_ref)
        else:
            chunk(k, z1_ref, a1_ref, z0_ref, a0_ref)

    @pl.when(e == n_steps - 1)
    def _():
        acc_ref[n_chunks - 1] += jnp.dot(vt_ref[...], a1_ref[...], preferred_element_type=F32)
        half_gate = 0.5 * g2_ref[0]
        for k in range(n_chunks):
            rows = slice(k * PEER_CHUNK, (k + 1) * PEER_CHUNK)
            y = acc_ref[k].T
            o_ref[0, rows, :] = _layernorm(alpha * x1_ref[0, rows, :] + half_gate * y,
                                           lng_ref[...], lnb_ref[...])


def _peer(hpt, u, vt, s2, g, kap, x1, mod3, ln_g, ln_b, alpha, tm, te):
    B, _, D, _ = hpt.shape
    T = x1.shape[1]
    E = u.shape[0]
    n_steps = E // te
    nc = tm // LANES
    nk = tm // PEER_CHUNK
    side = pl.BlockSpec((1, PEER_HEADS, nc, N_KEYS, LANES), lambda b, t, e: (b, 0, t, 1, 0))
    n_i = te // N_KEYS
    assert n_i in (4, 8) and nk % 2 == 0
    keyrows = pl.BlockSpec((1, PEER_HEADS, nc, 8, LANES), lambda b, t, e: (b, 0, t, (e * n_i) // 8, 0))
    return pl.pallas_call(
        functools.partial(_peer_kernel, alpha=alpha, n_steps=n_steps),
        grid=(B, T // tm, n_steps),
        in_specs=[pl.BlockSpec((1, nk, D, PEER_CHUNK), lambda b, t, e: (b, t, 0, 0)),
                  pl.BlockSpec((te, D), lambda b, t, e: (e, 0)),
                  pl.BlockSpec((te, D), lambda b, t, e: (jnp.minimum(e + 1, n_steps - 1), 0)),
                  pl.BlockSpec((D, te), lambda b, t, e: (0, e)),
                  pl.BlockSpec((D, te), lambda b, t, e: (0, jnp.maximum(e - 1, 0))),
                  side, keyrows,
                  pl.BlockSpec((1, PEER_HEADS, nc, 1, LANES), lambda b, t, e: (b, 0, t, 0, 0)),
                  pl.BlockSpec((1, tm, D), lambda b, t, e: (b, t, 0)),
                  pl.BlockSpec((1, 1, D), lambda b, t, e: (b * N_MOD + 5, 0, 0)),
                  pl.BlockSpec((1, D), lambda b, t, e: (0, 0)),
                  pl.BlockSpec((1, D), lambda b, t, e: (0, 0))],
        out_specs=pl.BlockSpec((1, tm, D), lambda b, t, e: (b, t, 0)),
        out_shape=jax.ShapeDtypeStruct((B, T, D), F32),
        scratch_shapes=[pltpu.VMEM((nk, D, PEER_CHUNK), F32),
                        pltpu.VMEM((te, PEER_CHUNK), F32), pltpu.VMEM((te, PEER_CHUNK), F32),
                        pltpu.VMEM((te, PEER_CHUNK), BF16), pltpu.VMEM((te, PEER_CHUNK), BF16)],
        compiler_params=_params(("parallel", "parallel", "arbitrary"), PEER_VMEM_LIMIT),
        name="peer",
    )(hpt, u, u, vt, vt, s2, g, kap, x1, mod3, ln_g, ln_b)


def _rope_tables(n_tokens):
    rows = n_tokens // GRID_W
    row = jnp.repeat(jnp.arange(rows, dtype=F32), GRID_W)
    col = jnp.tile(jnp.arange(GRID_W, dtype=F32), rows)
    n_freq = ATT_DH // 4
    inv = ROPE_THETA ** (-jnp.arange(n_freq, dtype=F32) / n_freq)
    ar, ac = row[:, None] * inv, col[:, None] * inv
    cos = jnp.concatenate([jnp.cos(ar), jnp.cos(ar), jnp.cos(ac), jnp.cos(ac)], axis=1)
    sin = jnp.concatenate([-jnp.sin(ar), jnp.sin(ar), -jnp.sin(ac), jnp.sin(ac)], axis=1)
    return jnp.tile(cos, (1, 2)), jnp.tile(sin, (1, 2))


def _dup_heads(w, n_heads, dh):
    return jnp.concatenate([w[:, h * dh:(h + 1) * dh] for h in range(n_heads) for _ in range(2)], axis=1)


def kernel(x, c, ctx, c_ctx, w_ada, b_ada, w_in, w_gate2_f, b_gate_f, w_gate2_b, b_gate_b, gla_norm_g,
           attn_sink, w_out, ln1_g, ln1_b, peer_wq, peer_subkeys, peer_u, peer_v, ln2_g, ln2_b):
    B, T, D = x.shape
    L = ctx.shape[1]
    depth = w_ada.shape[0]
    assert depth == 1, "single-layer block only"
    alpha = (2.0 * depth) ** 0.25

    cond = jnp.concatenate([jax.nn.silu(c), jax.nn.silu(c_ctx)[None]], axis=0)
    cond_bc = jnp.broadcast_to(cond[:, :, None], (B + 1, D, LANES))
    mod = _ada(cond_bc, w_ada[0], b_ada[0][None])
    mod3 = mod.reshape(8 * N_MOD, 1, D)

    wi = w_in[0]
    kq = 2 * GLA_HEADS * GLA_DK + 2 * GLA_HEADS * GLA_DV
    z0, a0 = kq, kq + 2 * GLA_RANK
    k0 = a0 + ATT_HEADS * ATT_DH
    v0 = k0 + ATT_KV_HEADS * ATT_DH
    w_p = jnp.concatenate([
        wi[:, :kq], wi[:, a0:k0],
        _dup_heads(wi[:, k0:v0], ATT_KV_HEADS, ATT_DH),
        _dup_heads(wi[:, v0:], ATT_KV_HEADS, ATT_DH),
        wi[:, z0:a0], jnp.zeros((D, LANES - 2 * GLA_RANK), F32)], axis=1).astype(BF16)
    gw = GLA_HEADS * GLA_DK
    w2 = jnp.zeros((LANES, 2 * gw), F32)
    w2 = w2.at[0:GLA_RANK, 0:gw].set(w_gate2_f[0]).at[GLA_RANK:2 * GLA_RANK, gw:].set(w_gate2_b[0]).astype(BF16)
    bg = jnp.concatenate([b_gate_f[0], b_gate_b[0]])[None]
    cos, sin = _rope_tables(T)
    ones = jnp.ones((L, LANES), F32)

    gq, gk, gv, r, g, aq, ak, av = _inproj(x, mod3, lambda b: b * N_MOD + 1, lambda b: b * N_MOD,
                                           w_p, w2, bg, cos, sin, tm=min(T, 512))
    _, cgk, cgv, _, cg, _, cak, cav = _inproj(ctx, mod3, lambda b: B * N_MOD + 1, lambda b: B * N_MOD,
                                              w_p, w2, bg, ones, 0.0 * ones, tm=L)

    of, ob = _gla(gq, gk, g, gv, cgk, cg, cgv, tt=min(T, 512))
    sink_bc = jnp.broadcast_to(attn_sink[0][:, None], (ATT_HEADS, LANES))
    att = _attn(aq, ak, av, cak, cav, sink_bc)

    sk = peer_subkeys[0]
    half = PEER_DQ // 2
    skb = jnp.zeros((PEER_HEADS, 2 * N_KEYS, PEER_DQ), F32)
    skb = skb.at[:, :N_KEYS, :half].set(sk[:, 0]).at[:, N_KEYS:, half:].set(sk[:, 1])
    skb = skb.astype(BF16)
    x1, hpt, st = _outproj(of, ob, r, att, x, mod3, w_out[0].astype(BF16), gla_norm_g[0][None],
                           ln1_g[0][None], ln1_b[0][None], peer_wq[0].T.astype(BF16), skb,
                           alpha, tm=min(T, 512))

    g, kap = _topk(st, tp=min(T, 512))
    u = peer_u[0].astype(BF16)
    vt = peer_v[0].T.astype(BF16)
    return _peer(hpt, u, vt, st, g, kap, x1, mod3, ln2_g[0][None], ln2_b[0][None],
                 alpha, tm=min(T, 1024), te=1024)
```

```python
import functools

import jax
import jax.numpy as jnp
from jax import lax
from jax.experimental import pallas as pl
from jax.experimental.pallas import tpu as pltpu

F32 = jnp.float32
BF16 = jnp.bfloat16

GRID_W = 64
GLA_HEADS = 4
GLA_DK = 64
GLA_DV = 128
GLA_RANK = 16
GLA_NORMALIZER = 16.0
GLA_CHUNK = 64
ATT_HEADS = 8
ATT_KV_HEADS = 2
ATT_DH = 64
WINDOW = 128
ATT_BLOCK = 128
ROPE_THETA = 10000.0
PEER_HEADS = 8
N_KEYS = 128
PEER_DQ = 128
PEER_TOPK = 16
N_MOD = 6
LN_EPS = 1e-5
NEG_INF = -1e30
LANES = 128
PEER_CHUNK = 256
LOG2E = 1.4426950408889634
MXU_TILE = 256

VMEM_LIMIT = 56 * 1024 * 1024
PEER_VMEM_LIMIT = 60 * 1024 * 1024

_NT = (((1,), (1,)), ((), ()))
_TN = (((0,), (0,)), ((), ()))


def _params(sem, vmem_limit=VMEM_LIMIT):
    return pltpu.CompilerParams(dimension_semantics=sem, vmem_limit_bytes=vmem_limit)


def _ada_kernel(s_ref, w_ref, b_ref, o_ref, *, n_rows):
    tn = w_ref.shape[1]
    w = w_ref[...]
    rows = []
    for r in range(n_rows):
        s = s_ref[r]
        cols = [jnp.sum(w[:, j * LANES:(j + 1) * LANES] * s, axis=0, keepdims=True)
                for j in range(tn // LANES)]
        rows.append(jnp.concatenate(cols, axis=1) + b_ref[...])
    rows.append(jnp.zeros((8 - n_rows, tn), F32))
    o_ref[...] = jnp.concatenate(rows, axis=0)


def _ada(cond_bc, w, b):
    n_rows, d, _ = cond_bc.shape
    n = w.shape[1]
    tn = 512
    return pl.pallas_call(
        functools.partial(_ada_kernel, n_rows=n_rows),
        grid=(n // tn,),
        in_specs=[pl.BlockSpec((n_rows, d, LANES), lambda j: (0, 0, 0)),
                  pl.BlockSpec((d, tn), lambda j: (0, j)),
                  pl.BlockSpec((1, tn), lambda j: (0, j))],
        out_specs=pl.BlockSpec((8, tn), lambda j: (0, j)),
        out_shape=jax.ShapeDtypeStruct((8, n), F32),
        compiler_params=_params(("parallel",)),
        name="ada",
    )(cond_bc, w, b)


def _rope(a, cos, sin):
    n = a.shape[1] // LANES
    c = jnp.concatenate([cos] * n, axis=1)
    s = jnp.concatenate([sin] * n, axis=1)
    lane = lax.broadcasted_iota(jnp.int32, a.shape, 1)
    swapped = jnp.where((lane % 32) < 16,
                        pltpu.roll(a, a.shape[1] - 16, 1),
                        pltpu.roll(a, 16, 1))
    return a * c + swapped * s


def _inproj_kernel(x_ref, sc_ref, sh_ref, w_ref, w2_ref, bg_ref, cos_ref, sin_ref,
                   gq_ref, gk_ref, gv_ref, r_ref, g_ref, aq_ref, ak_ref, av_ref):
    h = x_ref[0] * (1.0 + sc_ref[0]) + sh_ref[0]
    y = jnp.dot(h.astype(BF16), w_ref[...], preferred_element_type=F32)
    gq_ref[0] = y[:, 0:256] * (GLA_DK ** -0.5)
    gk_ref[0] = y[:, 256:512]
    gv_ref[0] = y[:, 512:1024].astype(BF16)
    r_ref[0] = y[:, 1024:1536]
    cos = cos_ref[...]
    sin = sin_ref[...]
    aq_ref[0] = (_rope(y[:, 1536:2048], cos, sin) * (ATT_DH ** -0.5)).astype(BF16)
    ak_ref[0] = _rope(y[:, 2048:2304], cos, sin).astype(BF16)
    av_ref[0] = y[:, 2304:2432].astype(BF16)
    z = y[:, 2432:2560]
    logits = jnp.dot(z.astype(BF16), w2_ref[...], preferred_element_type=F32) + bg_ref[...]
    g = (jnp.minimum(logits, 0.0) - jnp.log1p(jnp.exp(-jnp.abs(logits)))) * (1.0 / GLA_NORMALIZER)
    g_ref[0, 0] = g[:, 0:256]
    g_ref[1, 0] = g[:, 256:512]


def _inproj(x, mod3, sc_row, sh_row, w_p, w2, bg, cos, sin, tm):
    B, T, D = x.shape
    NP = w_p.shape[1]
    tok = lambda w, dt: jax.ShapeDtypeStruct((B, T, w), dt)
    tspec = lambda w: pl.BlockSpec((1, tm, w), lambda b, t: (b, t, 0))
    return pl.pallas_call(
        _inproj_kernel,
        grid=(B, T // tm),
        in_specs=[tspec(D),
                  pl.BlockSpec((1, 1, D), lambda b, t: (sc_row(b), 0, 0)),
                  pl.BlockSpec((1, 1, D), lambda b, t: (sh_row(b), 0, 0)),
                  pl.BlockSpec((D, NP), lambda b, t: (0, 0)),
                  pl.BlockSpec((LANES, 512), lambda b, t: (0, 0)),
                  pl.BlockSpec((1, 512), lambda b, t: (0, 0)),
                  pl.BlockSpec((tm, LANES), lambda b, t: (t, 0)),
                  pl.BlockSpec((tm, LANES), lambda b, t: (t, 0))],
        out_specs=[tspec(256), tspec(256), tspec(512), tspec(512),
                   pl.BlockSpec((2, 1, tm, 256), lambda b, t: (0, b, t, 0)),
                   tspec(512), tspec(256), tspec(128)],
        out_shape=[tok(256, F32), tok(256, F32), tok(512, BF16), tok(512, F32),
                   jax.ShapeDtypeStruct((2, B, T, 256), F32),
                   tok(512, BF16), tok(256, BF16), tok(128, BF16)],
        compiler_params=_params(("parallel", "parallel")),
        name="inproj",
    )(x, mod3, mod3, w_p, w2, bg, cos, sin)


def _gla_kernel(qf_ref, kf_ref, gf_ref, vf_ref, qb_ref, kb_ref, gb_ref, vb_ref, ck_ref, cg_ref, cv_ref,
                of_ref, ob_ref, st_ref, *, n_chunks, n_ctx_chunks):
    t = pl.program_id(0)
    nb = qf_ref.shape[0]
    C = GLA_CHUNK
    npairs = GLA_HEADS // 2
    ri = lax.broadcasted_iota(jnp.int32, (C, C), 0)
    ci = lax.broadcasted_iota(jnp.int32, (C, C), 1)
    tri = [ri >= ci, ri <= ci]
    tri_bf = [m.astype(BF16) for m in tri]
    tri2 = [jnp.concatenate([m, m], axis=0) for m in tri]
    lane = lax.broadcasted_iota(jnp.int32, (1, LANES), 1)
    m0 = (lane < GLA_DK).astype(F32)
    m1 = 1.0 - m0
    rr = lax.broadcasted_iota(jnp.int32, (2 * GLA_DV, LANES), 0)
    ll = lax.broadcasted_iota(jnp.int32, (2 * GLA_DV, LANES), 1)
    blockdiag = (rr < GLA_DV) == (ll < GLA_DK)

    def chunks(items, with_out):
        bs = []
        for (_, d, _), _, gc, _, _ in items:
            ghi = gc.astype(BF16)
            glo = (gc - ghi.astype(F32)).astype(BF16)
            bs.append(jnp.dot(tri_bf[d], ghi, preferred_element_type=F32)
                      + jnp.dot(tri_bf[d], glo, preferred_element_type=F32))
        btots = [jnp.sum(it[2], axis=0, keepdims=True) for it in items]
        sts = [st_ref[it[0]] for it in items]
        outs = [None] * len(items)
        if with_out:
            qes = [it[4] * jnp.exp(b) for it, b in zip(items, bs)]
            amats = []
            for it, b, qe in zip(items, bs, qes):
                ke = (it[1] * jnp.exp(-b)).astype(BF16)
                lhs = jnp.concatenate([qe * m0, qe * m1], axis=0).astype(BF16)
                amats.append(lax.dot_general(lhs, ke, _NT, preferred_element_type=F32))
            inters = [lax.dot_general(qe.astype(BF16), st.astype(BF16), _NT, preferred_element_type=F32)
                      for qe, st in zip(qes, sts)]
            for n, (it, a) in enumerate(zip(items, amats)):
                a = jnp.where(tri2[it[0][1]], a, 0.0).astype(BF16)
                oi = jnp.dot(a, it[3], preferred_element_type=F32)
                outs[n] = jnp.concatenate([oi[:C, :GLA_DV], oi[C:, GLA_DV:]], axis=1) + inters[n]
        for it, b, btot, st in zip(items, bs, btots, sts):
            kd = (it[1] * jnp.exp(btot - b)).astype(BF16)
            mt = lax.dot_general(it[3], kd, _TN, preferred_element_type=F32)
            st_ref[it[0]] = st * jnp.exp(btot) + jnp.where(blockdiag, mt, 0.0)
        return outs

    def qk_lanes(p):
        return slice(p * LANES, (p + 1) * LANES)

    def v_lanes(p):
        return slice(p * 2 * GLA_DV, (p + 1) * 2 * GLA_DV)

    @pl.when(t == 0)
    def _():
        st_ref[...] = jnp.zeros_like(st_ref)
        for j in range(n_ctx_chunks):
            items = []
            for bi in range(nb):
                for d in range(2):
                    cj = j if d == 0 else n_ctx_chunks - 1 - j
                    rows = slice(cj * C, (cj + 1) * C)
                    items += [((bi, d, p), ck_ref[bi, rows, qk_lanes(p)], cg_ref[d, bi, rows, qk_lanes(p)],
                               cv_ref[bi, rows, v_lanes(p)], None) for p in range(npairs)]
            chunks(items, False)

    streams = [(qf_ref, kf_ref, gf_ref, vf_ref, of_ref), (qb_ref, kb_ref, gb_ref, vb_ref, ob_ref)]

    def body(j, carry):
        items, dests = [], []
        for bi in range(nb):
            for d, (q_ref, k_ref, g_ref, v_ref, o_ref) in enumerate(streams):
                cj = j if d == 0 else n_chunks - 1 - j
                rows = pl.ds(pl.multiple_of(cj * C, C), C)
                for p in range(npairs):
                    items.append(((bi, d, p), k_ref[bi, rows, qk_lanes(p)], g_ref[0, bi, rows, qk_lanes(p)],
                                  v_ref[bi, rows, v_lanes(p)], q_ref[bi, rows, qk_lanes(p)]))
                    dests.append((o_ref, bi, rows, v_lanes(p)))
        for (o_ref, bi, rows, lanes), out in zip(dests, chunks(items, True)):
            o_ref[bi, rows, lanes] = out
        return carry

    lax.fori_loop(0, n_chunks, body, 0)


def _gla(gq, gk, g, gv, ck, cg, cv, tt):
    B, T, W = gq.shape
    L = ck.shape[1]
    nT = T // tt
    VW = gv.shape[2]
    fwd = lambda t: (0, t, 0)
    bwd = lambda t: (0, nT - 1 - t, 0)
    ctx = lambda t: (0, 0, 0)
    out = jax.ShapeDtypeStruct((B, T, VW), F32)
    return pl.pallas_call(
        functools.partial(_gla_kernel, n_chunks=tt // GLA_CHUNK, n_ctx_chunks=L // GLA_CHUNK),
        grid=(nT,),
        in_specs=[pl.BlockSpec((B, tt, W), fwd), pl.BlockSpec((B, tt, W), fwd),
                  pl.BlockSpec((1, B, tt, W), lambda t: (0, 0, t, 0)),
                  pl.BlockSpec((B, tt, VW), fwd),
                  pl.BlockSpec((B, tt, W), bwd), pl.BlockSpec((B, tt, W), bwd),
                  pl.BlockSpec((1, B, tt, W), lambda t: (1, 0, nT - 1 - t, 0)),
                  pl.BlockSpec((B, tt, VW), bwd),
                  pl.BlockSpec((B, L, W), ctx),
                  pl.BlockSpec((2, B, L, W), lambda t: (0, 0, 0, 0)),
                  pl.BlockSpec((B, L, VW), ctx)],
        out_specs=[pl.BlockSpec((B, tt, VW), fwd), pl.BlockSpec((B, tt, VW), bwd)],
        out_shape=[out, out],
        scratch_shapes=[pltpu.VMEM((B, 2, GLA_HEADS // 2, 2 * GLA_DV, LANES), F32)],
        compiler_params=_params(("arbitrary",)),
        name="gla",
    )(gq, gk, g, gv, gq, gk, g, gv, ck, cg, cv)


def _attn_kernel(q_ref, kp_ref, kc_ref, kn_ref, vp_ref, vc_ref, vn_ref, ck_ref, cv_ref, sink_ref,
                 o_ref, *, n_blocks):
    n = pl.program_id(1)
    BLK = ATT_BLOCK
    L = ck_ref.shape[1]
    nk = 3 * BLK + L
    group = ATT_HEADS // ATT_KV_HEADS
    ki = lax.broadcasted_iota(jnp.int32, (nk, group * BLK), 0)
    qi = lax.broadcasted_iota(jnp.int32, (nk, group * BLK), 1) & (BLK - 1)
    band = jnp.abs(ki - BLK - qi) <= WINDOW
    prev_ok = jnp.logical_or(ki >= BLK, n > 0)
    next_ok = jnp.logical_or(ki < 2 * BLK, n < n_blocks - 1)
    valid = jnp.logical_or(ki >= 3 * BLK, band & prev_ok & next_ok)
    lane = lax.broadcasted_iota(jnp.int32, (BLK, LANES), 1)
    zero = jnp.zeros((), BF16)
    items = range(ATT_HEADS // 2)
    width = 2 * BLK

    def k_rows(gi):
        kv = 2 * gi // group
        ks = slice(kv * LANES, (kv + 1) * LANES)
        return jnp.concatenate([r[0, :, ks] for r in (kp_ref, kc_ref, kn_ref, ck_ref)], axis=0)

    v_rows = jnp.concatenate([r[0] for r in (vp_ref, vc_ref, vn_ref, cv_ref)], axis=0)

    ok = valid[:, :width]
    scores, snks = [], []
    for gi in items:
        q2 = q_ref[0, :, gi * LANES:(gi + 1) * LANES]
        qstack = jnp.concatenate([jnp.where(lane < ATT_DH, q2, zero), jnp.where(lane >= ATT_DH, q2, zero)],
                                 axis=0)
        s = lax.dot_general(k_rows(gi), qstack, _NT,
                            preferred_element_type=F32)
        scores.append(jnp.where(ok, s, NEG_INF))
        snks.append(jnp.concatenate([sink_ref[2 * gi + hh:2 * gi + hh + 1, :] for hh in range(2)], axis=1))
    maxes = [jnp.maximum(jnp.max(s, axis=0, keepdims=True), snk) for s, snk in zip(scores, snks)]
    probs = [jnp.exp(s - m) for s, m in zip(scores, maxes)]
    dens = [jnp.sum(p, axis=0, keepdims=True) + jnp.exp(snk - m) for p, snk, m in zip(probs, snks, maxes)]
    outs = [lax.dot_general(v_rows, p.astype(BF16), _TN, preferred_element_type=F32)
            for p in probs]
    for gi, ot, den in zip(items, outs, dens):
        kv = 2 * gi // group
        ot = ot[kv * ATT_DH:(kv + 1) * ATT_DH] / den
        pair = jnp.concatenate([ot[:, :BLK], ot[:, BLK:]], axis=0)
        o_ref[0, :, gi * LANES:(gi + 1) * LANES] = pair.T.astype(o_ref.dtype)


def _attn(aq, ak, av, ck, cv, sink_bc):
    B, T, _ = aq.shape
    L = ck.shape[1]
    nb = T // ATT_BLOCK
    kw, vw = ak.shape[2], av.shape[2]
    cur = lambda b, n: (b, n, 0)
    prev = lambda b, n: (b, jnp.maximum(n - 1, 0), 0)
    nxt = lambda b, n: (b, jnp.minimum(n + 1, nb - 1), 0)
    kspec = lambda im: pl.BlockSpec((1, ATT_BLOCK, kw), im)
    vspec = lambda im: pl.BlockSpec((1, ATT_BLOCK, vw), im)
    return pl.pallas_call(
        functools.partial(_attn_kernel, n_blocks=nb),
        grid=(B, nb),
        in_specs=[pl.BlockSpec((1, ATT_BLOCK, ATT_HEADS * ATT_DH), cur),
                  kspec(prev), kspec(cur), kspec(nxt), vspec(prev), vspec(cur), vspec(nxt),
                  pl.BlockSpec((1, L, kw), lambda b, n: (b, 0, 0)),
                  pl.BlockSpec((1, L, vw), lambda b, n: (b, 0, 0)),
                  pl.BlockSpec((ATT_HEADS, LANES), lambda b, n: (0, 0))],
        out_specs=pl.BlockSpec((1, ATT_BLOCK, ATT_HEADS * ATT_DH), cur),
        out_shape=jax.ShapeDtypeStruct((B, T, ATT_HEADS * ATT_DH), BF16),
        compiler_params=_params(("parallel", "parallel")),
        name="attn",
    )(aq, ak, ak, ak, av, av, av, ck, cv, sink_bc)


def _layernorm(v, g, b):
    mu = jnp.mean(v, axis=-1, keepdims=True)
    vc = v - mu
    var = jnp.mean(vc * vc, axis=-1, keepdims=True)
    return vc * lax.rsqrt(var + LN_EPS) * g + b


def _outproj_kernel(of_ref, ob_ref, r_ref, att_ref, x_ref, g1_ref, sh2_ref, sc2_ref,
                    wout_ref, gng_ref, lng_ref, lnb_ref, wqt_ref, sk_ref,
                    x1_ref, hpt_ref, st_ref, *, alpha):
    o = of_ref[0] + ob_ref[0]
    r = r_ref[0]
    gng = gng_ref[...]
    parts = []
    for h in range(GLA_HEADS):
        oh = o[:, h * GLA_DV:(h + 1) * GLA_DV]
        oh = oh * lax.rsqrt(jnp.mean(oh * oh, axis=-1, keepdims=True) + LN_EPS) * gng
        rh = r[:, h * GLA_DV:(h + 1) * GLA_DV]
        parts.append((oh * (rh * jax.nn.sigmoid(rh))).astype(BF16))
    parts.append(att_ref[0])
    cat = jnp.concatenate(parts, axis=1)
    y = jnp.dot(cat, wout_ref[...], preferred_element_type=F32)
    x1 = _layernorm(alpha * x_ref[0] + g1_ref[0] * y, lng_ref[...], lnb_ref[...])
    x1_ref[0] = x1
    hp = x1 * (1.0 + sc2_ref[0]) + sh2_ref[0]
    hpt = hp.T.astype(BF16)
    for c in range(hpt.shape[1] // PEER_CHUNK):
        hpt_ref[0, c] = hpt[:, c * PEER_CHUNK:(c + 1) * PEER_CHUNK]
    qt = jnp.dot(wqt_ref[...], hpt, preferred_element_type=F32)
    for h in range(PEER_HEADS):
        qh = qt[h * PEER_DQ:(h + 1) * PEER_DQ, :].astype(BF16)
        s = jnp.dot(sk_ref[h], qh, preferred_element_type=F32) * LOG2E
        for c in range(s.shape[1] // LANES):
            st_ref[0, h, c] = s[:, c * LANES:(c + 1) * LANES]


def _outproj(of, ob, r, att, x, mod3, w_out, gng, ln_g, ln_b, wqt, skb, alpha, tm):
    B, T, D = x.shape
    W = of.shape[2]
    tok = lambda w: pl.BlockSpec((1, tm, w), lambda b, t: (b, t, 0))
    modrow = lambda j: pl.BlockSpec((1, 1, D), lambda b, t: (b * N_MOD + j, 0, 0))
    full = lambda a: pl.BlockSpec(a.shape, lambda b, t: (0,) * a.ndim)
    return pl.pallas_call(
        functools.partial(_outproj_kernel, alpha=alpha),
        grid=(B, T // tm),
        in_specs=[tok(W), tok(W), tok(W), tok(att.shape[2]), tok(D), modrow(2), modrow(3), modrow(4),
                  full(w_out), full(gng), full(ln_g), full(ln_b), full(wqt), full(skb)],
        out_specs=[tok(D),
                   pl.BlockSpec((1, tm // PEER_CHUNK, D, PEER_CHUNK), lambda b, t: (b, t, 0, 0)),
                   pl.BlockSpec((1, PEER_HEADS, tm // LANES, 2 * N_KEYS, LANES), lambda b, t: (b, 0, t, 0, 0))],
        out_shape=[jax.ShapeDtypeStruct((B, T, D), F32),
                   jax.ShapeDtypeStruct((B, T // PEER_CHUNK, D, PEER_CHUNK), BF16),
                   jax.ShapeDtypeStruct((B, PEER_HEADS, T // LANES, 2 * N_KEYS, LANES), F32)],
        compiler_params=_params(("parallel", "parallel")),
        name="outproj",
    )(of, ob, r, att, x, mod3, mod3, mod3, w_out, gng, ln_g, ln_b, wqt, skb)


def _sort_network(n):
    pairs = []

    def merge(lo, hi, r):
        step = 2 * r
        if step < hi - lo:
            merge(lo, hi, step)
            merge(lo + r, hi, step)
            pairs.extend((i, i + r) for i in range(lo + r, hi - r, step))
        else:
            pairs.append((lo, lo + r))

    def sort(lo, hi):
        if hi - lo >= 1:
            mid = lo + (hi - lo) // 2
            sort(lo, mid)
            sort(mid + 1, hi)
            merge(lo, hi, 1)

    sort(0, n - 1)
    return pairs


_SLABS = 16
_NET = _sort_network(_SLABS)


def _largest_desc(slabs, count):
    s = list(slabs) + [None] * (_SLABS - len(slabs))
    for i, j in _NET:
        if s[j] is None:
            continue
        if s[i] is None:
            s[i], s[j] = s[j], None
        else:
            s[i], s[j] = jnp.maximum(s[i], s[j]), jnp.minimum(s[i], s[j])
    depth = len(slabs)
    assert all(x is not None for x in s[:depth]) and all(x is None for x in s[depth:])
    vals = []
    for r in range(count):
        m = jnp.max(s[0], axis=0, keepdims=True)
        vals.append(m)
        hit = s[0] == m
        for l in range(min(depth, count - r - 1)):
            s[l] = jnp.where(hit, s[l + 1] if l + 1 < depth else NEG_INF, s[l])
    return vals


def _slabs(v):
    return [v[8 * k:8 * (k + 1)] for k in range(v.shape[0] // 8)]


def _topk_kernel(s_ref, g_ref, kap_ref):
    K = PEER_TOPK
    s = jnp.concatenate([s_ref[0, 0, c] for c in range(s_ref.shape[2])], axis=1)
    s1 = s[0:N_KEYS]
    s2 = s[N_KEYS:2 * N_KEYS]
    a = _largest_desc(_slabs(s1), K + 1)
    b = _largest_desc(_slabs(s2), K + 1)
    bmat = jnp.concatenate(b[:K], axis=0)
    row8 = lax.broadcasted_iota(jnp.int32, (8, s1.shape[1]), 0)
    cands = _slabs(a[0] + bmat)
    for r in range(1, K // 2):
        cands.append(jnp.where(row8 < (K + 1) // (r + 1), a[r] + bmat[0:8], NEG_INF))
    cands.append(jnp.concatenate(a[K // 2:K], axis=0) + b[0])
    cands.append(jnp.where(row8 == 0, a[0] + b[K], jnp.where(row8 == 1, a[K] + b[0], NEG_INF)))
    top = _largest_desc(cands, K + 1)
    tau = 0.5 * (top[K - 1] + top[K])
    mx = top[0]
    zsum = top[0] * 0.0
    for r in range(K):
        zsum = zsum + jnp.exp2(top[r] - mx)
    shift = mx + jnp.log2(zsum)
    g = s1 - shift
    kap = tau - shift
    for ci in range(s1.shape[1] // LANES):
        cols = slice(ci * LANES, (ci + 1) * LANES)
        g_ref[0, 0, ci] = g[:, cols]
        kap_ref[0, 0, ci] = kap[:, cols]


def _topk(st, tp):
    B, H, nchunks, _, _ = st.shape
    out = jax.ShapeDtypeStruct((B, H, nchunks, N_KEYS, LANES), F32)
    ospec = pl.BlockSpec((1, 1, tp // LANES, N_KEYS, LANES), lambda b, h, t: (b, h, t, 0, 0))
    return pl.pallas_call(
        _topk_kernel,
        grid=(B, H, nchunks * LANES // tp),
        in_specs=[pl.BlockSpec((1, 1, tp // LANES, 2 * N_KEYS, LANES), lambda b, h, t: (b, h, t, 0, 0))],
        out_specs=[ospec,
                   pl.BlockSpec((1, 1, tp // LANES, 1, LANES), lambda b, h, t: (b, h, t, 0, 0))],
        out_shape=[out, jax.ShapeDtypeStruct((B, H, nchunks, 1, LANES), F32)],
        compiler_params=_params(("parallel", "parallel", "parallel")),
        name="topk",
    )(st)


def _peer_kernel(hpt_ref, u_ref, un_ref, vt_ref, vp_ref, s2_ref, g_ref, kap_ref, x1_ref, g2_ref,
                 lng_ref, lnb_ref, o_ref, acc_ref, z0_ref, z1_ref, a0_ref, a1_ref, *, alpha, n_steps):
    e = pl.program_id(2)
    n_chunks = acc_ref.shape[0]
    te = u_ref.shape[0]
    n_i = te // N_KEYS
    sub = PEER_CHUNK // LANES
    upper = (e * n_i) % 8 >= n_i

    @pl.when(e == 0)
    def _():
        acc_ref[...] = jnp.zeros_like(acc_ref)
        a1_ref[...] = jnp.zeros_like(a1_ref)
        z0_ref[...] = jnp.dot(u_ref[...], hpt_ref[0, 0], preferred_element_type=F32)

    def chunk(k, z_cur, a_cur, z_next, a_prev):
        def next_scores():
            if k + 1 < n_chunks:
                z_next[...] = jnp.dot(u_ref[...], hpt_ref[0, k + 1], preferred_element_type=F32)
            else:
                z_next[...] = jnp.dot(un_ref[...], hpt_ref[0, 0], preferred_element_type=F32)

        def prev_output():
            if k > 0:
                acc_ref[k - 1] += jnp.dot(vt_ref[...], a_prev[...], preferred_element_type=F32)
            else:
                acc_ref[n_chunks - 1] += jnp.dot(vp_ref[...], a_prev[...], preferred_element_type=F32)

        def tile(ii, cc):
            rows = slice(ii * N_KEYS, (ii + 1) * N_KEYS)
            cols = slice(cc * LANES, (cc + 1) * LANES)
            c = k * sub + cc
            w = jnp.zeros((N_KEYS, LANES), F32)
            for h in range(PEER_HEADS):
                grow = g_ref[0, h, c, ii:ii + 1, :]
                if 2 * n_i <= 8:
                    grow = jnp.where(upper, g_ref[0, h, c, n_i + ii:n_i + ii + 1, :], grow)
                t = s2_ref[0, h, c] + grow
                w = w + jnp.exp2(jnp.where(t >= kap_ref[0, h, c], t, NEG_INF))
            zt = z_cur[rows, cols]
            act = zt * (1.0 + lax.erf(zt * (2.0 ** -0.5)))
            a_cur[rows, cols] = (w * act).astype(BF16)

        tiles = [(ii, cc) for ii in range(n_i) for cc in range(sub)]
        prev_output()
        for ii, cc in tiles[:len(tiles) // 2]:
            tile(ii, cc)
        next_scores()
        for ii, cc in tiles[len(tiles) // 2:]:
            tile(ii, cc)

    for k in range(n_chunks):
        if k % 2 == 0:
            chunk(k, z0_ref, a0_ref, z1_ref, a1_ref)
        else:
            chunk(k, z1_ref, a1_ref, z0_ref, a0_ref)

    @pl.when(e == n_steps - 1)
    def _():
        acc_ref[n_chunks - 1] += jnp.dot(vt_ref[...], a1_ref[...], preferred_element_type=F32)
        half_gate = 0.5 * g2_ref[0]
        for k in range(n_chunks):
            rows = slice(k * PEER_CHUNK, (k + 1) * PEER_CHUNK)
            y = acc_ref[k].T
            o_ref[0, rows, :] = _layernorm(alpha * x1_ref[0, rows, :] + half_gate * y,
                                           lng_ref[...], lnb_ref[...])


def _peer(hpt, u, vt, s2, g, kap, x1, mod3, ln_g, ln_b, alpha, tm, te):
    B, _, D, _ = hpt.shape
    T = x1.shape[1]
    E = u.shape[0]
    n_steps = E // te
    nc = tm // LANES
    nk = tm // PEER_CHUNK
    side = pl.BlockSpec((1, PEER_HEADS, nc, N_KEYS, LANES), lambda b, t, e: (b, 0, t, 1, 0))
    n_i = te // N_KEYS
    assert n_i in (4, 8) and nk % 2 == 0
    keyrows = pl.BlockSpec((1, PEER_HEADS, nc, 8, LANES), lambda b, t, e: (b, 0, t, (e * n_i) // 8, 0))
    return pl.pallas_call(
        functools.partial(_peer_kernel, alpha=alpha, n_steps=n_steps),
        grid=(B, T // tm, n_steps),
        in_specs=[pl.BlockSpec((1, nk, D, PEER_CHUNK), lambda b, t, e: (b, t, 0, 0)),
                  pl.BlockSpec((te, D), lambda b, t, e: (e, 0)),
                  pl.BlockSpec((te, D), lambda b, t, e: (jnp.minimum(e + 1, n_steps - 1), 0)),
                  pl.BlockSpec((D, te), lambda b, t, e: (0, e)),
                  pl.BlockSpec((D, te), lambda b, t, e: (0, jnp.maximum(e - 1, 0))),
                  side, keyrows,
                  pl.BlockSpec((1, PEER_HEADS, nc, 1, LANES), lambda b, t, e: (b, 0, t, 0, 0)),
                  pl.BlockSpec((1, tm, D), lambda b, t, e: (b, t, 0)),
                  pl.BlockSpec((1, 1, D), lambda b, t, e: (b * N_MOD + 5, 0, 0)),
                  pl.BlockSpec((1, D), lambda b, t, e: (0, 0)),
                  pl.BlockSpec((1, D), lambda b, t, e: (0, 0))],
        out_specs=pl.BlockSpec((1, tm, D), lambda b, t, e: (b, t, 0)),
        out_shape=jax.ShapeDtypeStruct((B, T, D), F32),
        scratch_shapes=[pltpu.VMEM((nk, D, PEER_CHUNK), F32),
                        pltpu.VMEM((te, PEER_CHUNK), F32), pltpu.VMEM((te, PEER_CHUNK), F32),
                        pltpu.VMEM((te, PEER_CHUNK), BF16), pltpu.VMEM((te, PEER_CHUNK), BF16)],
        compiler_params=_params(("parallel", "parallel", "arbitrary"), PEER_VMEM_LIMIT),
        name="peer",
    )(hpt, u, u, vt, vt, s2, g, kap, x1, mod3, ln_g, ln_b)


def _rope_tables(n_tokens):
    rows = n_tokens // GRID_W
    row = jnp.repeat(jnp.arange(rows, dtype=F32), GRID_W)
    col = jnp.tile(jnp.arange(GRID_W, dtype=F32), rows)
    n_freq = ATT_DH // 4
    inv = ROPE_THETA ** (-jnp.arange(n_freq, dtype=F32) / n_freq)
    ar, ac = row[:, None] * inv, col[:, None] * inv
    cos = jnp.concatenate([jnp.cos(ar), jnp.cos(ar), jnp.cos(ac), jnp.cos(ac)], axis=1)
    sin = jnp.concatenate([-jnp.sin(ar), jnp.sin(ar), -jnp.sin(ac), jnp.sin(ac)], axis=1)
    return jnp.tile(cos, (1, 2)), jnp.tile(sin, (1, 2))


def _dup_heads(w, n_heads, dh):
    return jnp.concatenate([w[:, h * dh:(h + 1) * dh] for h in range(n_heads) for _ in range(2)], axis=1)


def kernel(x, c, ctx, c_ctx, w_ada, b_ada, w_in, w_gate2_f, b_gate_f, w_gate2_b, b_gate_b, gla_norm_g,
           attn_sink, w_out, ln1_g, ln1_b, peer_wq, peer_subkeys, peer_u, peer_v, ln2_g, ln2_b):
    B, T, D = x.shape
    L = ctx.shape[1]
    depth = w_ada.shape[0]
    assert depth == 1, "single-layer block only"
    alpha = (2.0 * depth) ** 0.25

    cond = jnp.concatenate([jax.nn.silu(c), jax.nn.silu(c_ctx)[None]], axis=0)
    cond_bc = jnp.broadcast_to(cond[:, :, None], (B + 1, D, LANES))
    mod = _ada(cond_bc, w_ada[0], b_ada[0][None])
    mod3 = mod.reshape(8 * N_MOD, 1, D)

    wi = w_in[0]
    kq = 2 * GLA_HEADS * GLA_DK + 2 * GLA_HEADS * GLA_DV
    z0, a0 = kq, kq + 2 * GLA_RANK
    k0 = a0 + ATT_HEADS * ATT_DH
    v0 = k0 + ATT_KV_HEADS * ATT_DH
    w_p = jnp.concatenate([
        wi[:, :kq], wi[:, a0:k0],
        _dup_heads(wi[:, k0:v0], ATT_KV_HEADS, ATT_DH),
        wi[:, v0:],
        wi[:, z0:a0], jnp.zeros((D, LANES - 2 * GLA_RANK), F32)], axis=1).astype(BF16)
    gw = GLA_HEADS * GLA_DK
    w2 = jnp.zeros((LANES, 2 * gw), F32)
    w2 = w2.at[0:GLA_RANK, 0:gw].set(w_gate2_f[0]).at[GLA_RANK:2 * GLA_RANK, gw:].set(w_gate2_b[0]).astype(BF16)
    bg = jnp.concatenate([b_gate_f[0], b_gate_b[0]])[None]
    cos, sin = _rope_tables(T)
    ones = jnp.ones((L, LANES), F32)

    gq, gk, gv, r, g, aq, ak, av = _inproj(x, mod3, lambda b: b * N_MOD + 1, lambda b: b * N_MOD,
                                           w_p, w2, bg, cos, sin, tm=min(T, 512))
    _, cgk, cgv, _, cg, _, cak, cav = _inproj(ctx, mod3, lambda b: B * N_MOD + 1, lambda b: B * N_MOD,
                                              w_p, w2, bg, ones, 0.0 * ones, tm=L)

    of, ob = _gla(gq, gk, g, gv, cgk, cg, cgv, tt=min(T, 512))
    sink_bc = jnp.broadcast_to(attn_sink[0][:, None], (ATT_HEADS, LANES))
    att = _attn(aq, ak, av, cak, cav, sink_bc)

    sk = peer_subkeys[0]
    half = PEER_DQ // 2
    skb = jnp.zeros((PEER_HEADS, 2 * N_KEYS, PEER_DQ), F32)
    skb = skb.at[:, :N_KEYS, :half].set(sk[:, 0]).at[:, N_KEYS:, half:].set(sk[:, 1])
    skb = skb.astype(BF16)
    x1, hpt, st = _outproj(of, ob, r, att, x, mod3, w_out[0].astype(BF16), gla_norm_g[0][None],
                           ln1_g[0][None], ln1_b[0][None], peer_wq[0].T.astype(BF16), skb,
                           alpha, tm=min(T, 512))

    g, kap = _topk(st, tp=min(T, 512))
    u = peer_u[0].astype(BF16)
    vt = peer_v[0].T.astype(BF16)
    return _peer(hpt, u, vt, st, g, kap, x1, mod3, ln2_g[0][None], ln2_b[0][None],
                 alpha, tm=min(T, 1024), te=1024)
```

```python
import functools

import jax
import jax.numpy as jnp
from jax import lax
from jax.experimental import pallas as pl
from jax.experimental.pallas import tpu as pltpu

F32 = jnp.float32
BF16 = jnp.bfloat16

GRID_W = 64
GLA_HEADS = 4
GLA_DK = 64
GLA_DV = 128
GLA_RANK = 16
GLA_NORMALIZER = 16.0
GLA_CHUNK = 64
ATT_HEADS = 8
ATT_KV_HEADS = 2
ATT_DH = 64
WINDOW = 128
ATT_BLOCK = 128
ROPE_THETA = 10000.0
PEER_HEADS = 8
N_KEYS = 128
PEER_DQ = 128
PEER_TOPK = 16
N_MOD = 6
LN_EPS = 1e-5
NEG_INF = -1e30
LANES = 128
PEER_CHUNK = 256
LOG2E = 1.4426950408889634
MXU_TILE = 256

VMEM_LIMIT = 56 * 1024 * 1024
PEER_VMEM_LIMIT = 60 * 1024 * 1024

_NT = (((1,), (1,)), ((), ()))
_TN = (((0,), (0,)), ((), ()))


def _params(sem, vmem_limit=VMEM_LIMIT):
    return pltpu.CompilerParams(dimension_semantics=sem, vmem_limit_bytes=vmem_limit)


def _ada_kernel(s_ref, w_ref, b_ref, o_ref, *, n_rows):
    tn = w_ref.shape[1]
    w = w_ref[...]
    rows = []
    for r in range(n_rows):
        s = s_ref[r]
        cols = [jnp.sum(w[:, j * LANES:(j + 1) * LANES] * s, axis=0, keepdims=True)
                for j in range(tn // LANES)]
        rows.append(jnp.concatenate(cols, axis=1) + b_ref[...])
    rows.append(jnp.zeros((8 - n_rows, tn), F32))
    o_ref[...] = jnp.concatenate(rows, axis=0)


def _ada(cond_bc, w, b):
    n_rows, d, _ = cond_bc.shape
    n = w.shape[1]
    tn = 1024
    return pl.pallas_call(
        functools.partial(_ada_kernel, n_rows=n_rows),
        grid=(n // tn,),
        in_specs=[pl.BlockSpec((n_rows, d, LANES), lambda j: (0, 0, 0)),
                  pl.BlockSpec((d, tn), lambda j: (0, j)),
                  pl.BlockSpec((1, tn), lambda j: (0, j))],
        out_specs=pl.BlockSpec((8, tn), lambda j: (0, j)),
        out_shape=jax.ShapeDtypeStruct((8, n), F32),
        compiler_params=_params(("parallel",)),
        name="ada",
    )(cond_bc, w, b)


def _rope(a, cos, sin):
    n = a.shape[1] // LANES
    c = jnp.concatenate([cos] * n, axis=1)
    s = jnp.concatenate([sin] * n, axis=1)
    lane = lax.broadcasted_iota(jnp.int32, a.shape, 1)
    swapped = jnp.where((lane % 32) < 16,
                        pltpu.roll(a, a.shape[1] - 16, 1),
                        pltpu.roll(a, 16, 1))
    return a * c + swapped * s


def _inproj_kernel(x_ref, sc_ref, sh_ref, w_ref, w2_ref, bg_ref, cos_ref, sin_ref,
                   gq_ref, gk_ref, gv_ref, r_ref, g_ref, aq_ref, ak_ref, av_ref):
    h = x_ref[0] * (1.0 + sc_ref[0]) + sh_ref[0]
    y = jnp.dot(h.astype(BF16), w_ref[...], preferred_element_type=F32)
    gq_ref[0] = y[:, 0:256] * (GLA_DK ** -0.5)
    gk_ref[0] = y[:, 256:512]
    gv_ref[0] = y[:, 512:1024].astype(BF16)
    r_ref[0] = y[:, 1024:1536]
    cos = cos_ref[...]
    sin = sin_ref[...]
    aq_ref[0] = (_rope(y[:, 1536:2048], cos, sin) * (ATT_DH ** -0.5)).astype(BF16)
    ak_ref[0] = _rope(y[:, 2048:2304], cos, sin).astype(BF16)
    av_ref[0] = y[:, 2304:2432].astype(BF16)
    z = y[:, 2432:2560]
    logits = jnp.dot(z.astype(BF16), w2_ref[...], preferred_element_type=F32) + bg_ref[...]
    g = (jnp.minimum(logits, 0.0) - jnp.log1p(jnp.exp(-jnp.abs(logits)))) * (1.0 / GLA_NORMALIZER)
    g_ref[0, 0] = g[:, 0:256]
    g_ref[1, 0] = g[:, 256:512]


def _inproj(x, mod3, sc_row, sh_row, w_p, w2, bg, cos, sin, tm):
    B, T, D = x.shape
    NP = w_p.shape[1]
    tok = lambda w, dt: jax.ShapeDtypeStruct((B, T, w), dt)
    tspec = lambda w: pl.BlockSpec((1, tm, w), lambda b, t: (b, t, 0))
    return pl.pallas_call(
        _inproj_kernel,
        grid=(B, T // tm),
        in_specs=[tspec(D),
                  pl.BlockSpec((1, 1, D), lambda b, t: (sc_row(b), 0, 0)),
                  pl.BlockSpec((1, 1, D), lambda b, t: (sh_row(b), 0, 0)),
                  pl.BlockSpec((D, NP), lambda b, t: (0, 0)),
                  pl.BlockSpec((LANES, 512), lambda b, t: (0, 0)),
                  pl.BlockSpec((1, 512), lambda b, t: (0, 0)),
                  pl.BlockSpec((tm, LANES), lambda b, t: (t, 0)),
                  pl.BlockSpec((tm, LANES), lambda b, t: (t, 0))],
        out_specs=[tspec(256), tspec(256), tspec(512), tspec(512),
                   pl.BlockSpec((2, 1, tm, 256), lambda b, t: (0, b, t, 0)),
                   tspec(512), tspec(256), tspec(128)],
        out_shape=[tok(256, F32), tok(256, F32), tok(512, BF16), tok(512, F32),
                   jax.ShapeDtypeStruct((2, B, T, 256), F32),
                   tok(512, BF16), tok(256, BF16), tok(128, BF16)],
        compiler_params=_params(("parallel", "parallel")),
        name="inproj",
    )(x, mod3, mod3, w_p, w2, bg, cos, sin)


def _gla_kernel(qf_ref, kf_ref, gf_ref, vf_ref, qb_ref, kb_ref, gb_ref, vb_ref, ck_ref, cg_ref, cv_ref,
                of_ref, ob_ref, st_ref, *, n_chunks, n_ctx_chunks):
    t = pl.program_id(0)
    nb = qf_ref.shape[0]
    C = GLA_CHUNK
    npairs = GLA_HEADS // 2
    ri = lax.broadcasted_iota(jnp.int32, (C, C), 0)
    ci = lax.broadcasted_iota(jnp.int32, (C, C), 1)
    tri = [ri >= ci, ri <= ci]
    tri_bf = [m.astype(BF16) for m in tri]
    tri2 = [jnp.concatenate([m, m], axis=0) for m in tri]
    lane = lax.broadcasted_iota(jnp.int32, (1, LANES), 1)
    m0 = (lane < GLA_DK).astype(F32)
    m1 = 1.0 - m0
    rr = lax.broadcasted_iota(jnp.int32, (2 * GLA_DV, LANES), 0)
    ll = lax.broadcasted_iota(jnp.int32, (2 * GLA_DV, LANES), 1)
    blockdiag = (rr < GLA_DV) == (ll < GLA_DK)

    def chunks(items, with_out):
        bs = []
        for (_, d, _), _, gc, _, _ in items:
            ghi = gc.astype(BF16)
            glo = (gc - ghi.astype(F32)).astype(BF16)
            bs.append(jnp.dot(tri_bf[d], ghi, preferred_element_type=F32)
                      + jnp.dot(tri_bf[d], glo, preferred_element_type=F32))
        btots = [jnp.sum(it[2], axis=0, keepdims=True) for it in items]
        sts = [st_ref[it[0]] for it in items]
        outs = [None] * len(items)
        if with_out:
            qes = [it[4] * jnp.exp(b) for it, b in zip(items, bs)]
            amats = []
            for it, b, qe in zip(items, bs, qes):
                ke = (it[1] * jnp.exp(-b)).astype(BF16)
                lhs = jnp.concatenate([qe * m0, qe * m1], axis=0).astype(BF16)
                amats.append(lax.dot_general(lhs, ke, _NT, preferred_element_type=F32))
            inters = [lax.dot_general(qe.astype(BF16), st.astype(BF16), _NT, preferred_element_type=F32)
                      for qe, st in zip(qes, sts)]
            for n, (it, a) in enumerate(zip(items, amats)):
                a = jnp.where(tri2[it[0][1]], a, 0.0).astype(BF16)
                oi = jnp.dot(a, it[3], preferred_element_type=F32)
                outs[n] = jnp.concatenate([oi[:C, :GLA_DV], oi[C:, GLA_DV:]], axis=1) + inters[n]
        for it, b, btot, st in zip(items, bs, btots, sts):
            kd = (it[1] * jnp.exp(btot - b)).astype(BF16)
            mt = lax.dot_general(it[3], kd, _TN, preferred_element_type=F32)
            st_ref[it[0]] = st * jnp.exp(btot) + jnp.where(blockdiag, mt, 0.0)
        return outs

    def qk_lanes(p):
        return slice(p * LANES, (p + 1) * LANES)

    def v_lanes(p):
        return slice(p * 2 * GLA_DV, (p + 1) * 2 * GLA_DV)

    @pl.when(t == 0)
    def _():
        st_ref[...] = jnp.zeros_like(st_ref)
        for j in range(n_ctx_chunks):
            items = []
            for bi in range(nb):
                for d in range(2):
                    cj = j if d == 0 else n_ctx_chunks - 1 - j
                    rows = slice(cj * C, (cj + 1) * C)
                    items += [((bi, d, p), ck_ref[bi, rows, qk_lanes(p)], cg_ref[d, bi, rows, qk_lanes(p)],
                               cv_ref[bi, rows, v_lanes(p)], None) for p in range(npairs)]
            chunks(items, False)

    streams = [(qf_ref, kf_ref, gf_ref, vf_ref, of_ref), (qb_ref, kb_ref, gb_ref, vb_ref, ob_ref)]

    def body(j, carry):
        items, dests = [], []
        for bi in range(nb):
            for d, (q_ref, k_ref, g_ref, v_ref, o_ref) in enumerate(streams):
                cj = j if d == 0 else n_chunks - 1 - j
                rows = pl.ds(pl.multiple_of(cj * C, C), C)
                for p in range(npairs):
                    items.append(((bi, d, p), k_ref[bi, rows, qk_lanes(p)], g_ref[0, bi, rows, qk_lanes(p)],
                                  v_ref[bi, rows, v_lanes(p)], q_ref[bi, rows, qk_lanes(p)]))
                    dests.append((o_ref, bi, rows, v_lanes(p)))
        for (o_ref, bi, rows, lanes), out in zip(dests, chunks(items, True)):
            o_ref[bi, rows, lanes] = out
        return carry

    lax.fori_loop(0, n_chunks, body, 0)


def _gla(gq, gk, g, gv, ck, cg, cv, tt):
    B, T, W = gq.shape
    L = ck.shape[1]
    nT = T // tt
    VW = gv.shape[2]
    fwd = lambda t: (0, t, 0)
    bwd = lambda t: (0, nT - 1 - t, 0)
    ctx = lambda t: (0, 0, 0)
    out = jax.ShapeDtypeStruct((B, T, VW), F32)
    return pl.pallas_call(
        functools.partial(_gla_kernel, n_chunks=tt // GLA_CHUNK, n_ctx_chunks=L // GLA_CHUNK),
        grid=(nT,),
        in_specs=[pl.BlockSpec((B, tt, W), fwd), pl.BlockSpec((B, tt, W), fwd),
                  pl.BlockSpec((1, B, tt, W), lambda t: (0, 0, t, 0)),
                  pl.BlockSpec((B, tt, VW), fwd),
                  pl.BlockSpec((B, tt, W), bwd), pl.BlockSpec((B, tt, W), bwd),
                  pl.BlockSpec((1, B, tt, W), lambda t: (1, 0, nT - 1 - t, 0)),
                  pl.BlockSpec((B, tt, VW), bwd),
                  pl.BlockSpec((B, L, W), ctx),
                  pl.BlockSpec((2, B, L, W), lambda t: (0, 0, 0, 0)),
                  pl.BlockSpec((B, L, VW), ctx)],
        out_specs=[pl.BlockSpec((B, tt, VW), fwd), pl.BlockSpec((B, tt, VW), bwd)],
        out_shape=[out, out],
        scratch_shapes=[pltpu.VMEM((B, 2, GLA_HEADS // 2, 2 * GLA_DV, LANES), F32)],
        compiler_params=_params(("arbitrary",)),
        name="gla",
    )(gq, gk, g, gv, gq, gk, g, gv, ck, cg, cv)


def _attn_kernel(q_ref, kp_ref, kc_ref, kn_ref, vp_ref, vc_ref, vn_ref, ck_ref, cv_ref, sink_ref,
                 o_ref, *, n_blocks):
    n = pl.program_id(1)
    BLK = ATT_BLOCK
    L = ck_ref.shape[1]
    nk = 3 * BLK + L
    group = ATT_HEADS // ATT_KV_HEADS
    ki = lax.broadcasted_iota(jnp.int32, (nk, group * BLK), 0)
    qi = lax.broadcasted_iota(jnp.int32, (nk, group * BLK), 1) & (BLK - 1)
    band = jnp.abs(ki - BLK - qi) <= WINDOW
    prev_ok = jnp.logical_or(ki >= BLK, n > 0)
    next_ok = jnp.logical_or(ki < 2 * BLK, n < n_blocks - 1)
    valid = jnp.logical_or(ki >= 3 * BLK, band & prev_ok & next_ok)
    lane = lax.broadcasted_iota(jnp.int32, (BLK, LANES), 1)
    zero = jnp.zeros((), BF16)
    items = range(ATT_HEADS // 2)
    width = 2 * BLK

    def k_rows(gi):
        kv = 2 * gi // group
        ks = slice(kv * LANES, (kv + 1) * LANES)
        return jnp.concatenate([r[0, :, ks] for r in (kp_ref, kc_ref, kn_ref, ck_ref)], axis=0)

    v_rows = jnp.concatenate([r[0] for r in (vp_ref, vc_ref, vn_ref, cv_ref)], axis=0)

    ok = valid[:, :width]
    scores, snks = [], []
    for gi in items:
        q2 = q_ref[0, :, gi * LANES:(gi + 1) * LANES]
        qstack = jnp.concatenate([jnp.where(lane < ATT_DH, q2, zero), jnp.where(lane >= ATT_DH, q2, zero)],
                                 axis=0)
        s = lax.dot_general(k_rows(gi), qstack, _NT,
                            preferred_element_type=F32)
        scores.append(jnp.where(ok, s, NEG_INF))
        snks.append(jnp.concatenate([sink_ref[2 * gi + hh:2 * gi + hh + 1, :] for hh in range(2)], axis=1))
    maxes = [jnp.maximum(jnp.max(s, axis=0, keepdims=True), snk) for s, snk in zip(scores, snks)]
    probs = [jnp.exp(s - m) for s, m in zip(scores, maxes)]
    dens = [jnp.sum(p, axis=0, keepdims=True) + jnp.exp(snk - m) for p, snk, m in zip(probs, snks, maxes)]
    outs = [lax.dot_general(v_rows, p.astype(BF16), _TN, preferred_element_type=F32)
            for p in probs]
    for gi, ot, den in zip(items, outs, dens):
        kv = 2 * gi // group
        ot = ot[kv * ATT_DH:(kv + 1) * ATT_DH] / den
        pair = jnp.concatenate([ot[:, :BLK], ot[:, BLK:]], axis=0)
        o_ref[0, :, gi * LANES:(gi + 1) * LANES] = pair.T.astype(o_ref.dtype)


def _attn(aq, ak, av, ck, cv, sink_bc):
    B, T, _ = aq.shape
    L = ck.shape[1]
    nb = T // ATT_BLOCK
    kw, vw = ak.shape[2], av.shape[2]
    cur = lambda b, n: (b, n, 0)
    prev = lambda b, n: (b, jnp.maximum(n - 1, 0), 0)
    nxt = lambda b, n: (b, jnp.minimum(n + 1, nb - 1), 0)
    kspec = lambda im: pl.BlockSpec((1, ATT_BLOCK, kw), im)
    vspec = lambda im: pl.BlockSpec((1, ATT_BLOCK, vw), im)
    return pl.pallas_call(
        functools.partial(_attn_kernel, n_blocks=nb),
        grid=(B, nb),
        in_specs=[pl.BlockSpec((1, ATT_BLOCK, ATT_HEADS * ATT_DH), cur),
                  kspec(prev), kspec(cur), kspec(nxt), vspec(prev), vspec(cur), vspec(nxt),
                  pl.BlockSpec((1, L, kw), lambda b, n: (b, 0, 0)),
                  pl.BlockSpec((1, L, vw), lambda b, n: (b, 0, 0)),
                  pl.BlockSpec((ATT_HEADS, LANES), lambda b, n: (0, 0))],
        out_specs=pl.BlockSpec((1, ATT_BLOCK, ATT_HEADS * ATT_DH), cur),
        out_shape=jax.ShapeDtypeStruct((B, T, ATT_HEADS * ATT_DH), BF16),
        compiler_params=_params(("parallel", "parallel")),
        name="attn",
    )(aq, ak, ak, ak, av, av, av, ck, cv, sink_bc)


def _layernorm(v, g, b):
    mu = jnp.mean(v, axis=-1, keepdims=True)
    vc = v - mu
    var = jnp.mean(vc * vc, axis=-1, keepdims=True)
    return vc * lax.rsqrt(var + LN_EPS) * g + b


def _outproj_kernel(of_ref, ob_ref, r_ref, att_ref, x_ref, g1_ref, sh2_ref, sc2_ref,
                    wout_ref, gng_ref, lng_ref, lnb_ref, wqt_ref, sk_ref,
                    x1_ref, hpt_ref, st_ref, *, alpha):
    o = of_ref[0] + ob_ref[0]
    r = r_ref[0]
    gng = gng_ref[...]
    parts = []
    for h in range(GLA_HEADS):
        oh = o[:, h * GLA_DV:(h + 1) * GLA_DV]
        oh = oh * lax.rsqrt(jnp.mean(oh * oh, axis=-1, keepdims=True) + LN_EPS) * gng
        rh = r[:, h * GLA_DV:(h + 1) * GLA_DV]
        parts.append((oh * (rh * jax.nn.sigmoid(rh))).astype(BF16))
    parts.append(att_ref[0])
    cat = jnp.concatenate(parts, axis=1)
    y = jnp.dot(cat, wout_ref[...], preferred_element_type=F32)
    x1 = _layernorm(alpha * x_ref[0] + g1_ref[0] * y, lng_ref[...], lnb_ref[...])
    x1_ref[0] = x1
    hp = x1 * (1.0 + sc2_ref[0]) + sh2_ref[0]
    hpt = hp.T.astype(BF16)
    for c in range(hpt.shape[1] // PEER_CHUNK):
        hpt_ref[0, c] = hpt[:, c * PEER_CHUNK:(c + 1) * PEER_CHUNK]
    qt = jnp.dot(wqt_ref[...], hpt, preferred_element_type=F32)
    for h in range(PEER_HEADS):
        qh = qt[h * PEER_DQ:(h + 1) * PEER_DQ, :].astype(BF16)
        s = jnp.dot(sk_ref[h], qh, preferred_element_type=F32) * LOG2E
        for c in range(s.shape[1] // LANES):
            st_ref[0, h, c] = s[:, c * LANES:(c + 1) * LANES]


def _outproj(of, ob, r, att, x, mod3, w_out, gng, ln_g, ln_b, wqt, skb, alpha, tm):
    B, T, D = x.shape
    W = of.shape[2]
    tok = lambda w: pl.BlockSpec((1, tm, w), lambda b, t: (b, t, 0))
    modrow = lambda j: pl.BlockSpec((1, 1, D), lambda b, t: (b * N_MOD + j, 0, 0))
    full = lambda a: pl.BlockSpec(a.shape, lambda b, t: (0,) * a.ndim)
    return pl.pallas_call(
        functools.partial(_outproj_kernel, alpha=alpha),
        grid=(B, T // tm),
        in_specs=[tok(W), tok(W), tok(W), tok(att.shape[2]), tok(D), modrow(2), modrow(3), modrow(4),
                  full(w_out), full(gng), full(ln_g), full(ln_b), full(wqt), full(skb)],
        out_specs=[tok(D),
                   pl.BlockSpec((1, tm // PEER_CHUNK, D, PEER_CHUNK), lambda b, t: (b, t, 0, 0)),
                   pl.BlockSpec((1, PEER_HEADS, tm // LANES, 2 * N_KEYS, LANES), lambda b, t: (b, 0, t, 0, 0))],
        out_shape=[jax.ShapeDtypeStruct((B, T, D), F32),
                   jax.ShapeDtypeStruct((B, T // PEER_CHUNK, D, PEER_CHUNK), BF16),
                   jax.ShapeDtypeStruct((B, PEER_HEADS, T // LANES, 2 * N_KEYS, LANES), F32)],
        compiler_params=_params(("parallel", "parallel")),
        name="outproj",
    )(of, ob, r, att, x, mod3, mod3, mod3, w_out, gng, ln_g, ln_b, wqt, skb)


def _sort_network(n):
    pairs = []

    def merge(lo, hi, r):
        step = 2 * r
        if step < hi - lo:
            merge(lo, hi, step)
            merge(lo + r, hi, step)
            pairs.extend((i, i + r) for i in range(lo + r, hi - r, step))
        else:
            pairs.append((lo, lo + r))

    def sort(lo, hi):
        if hi - lo >= 1:
            mid = lo + (hi - lo) // 2
            sort(lo, mid)
            sort(mid + 1, hi)
            merge(lo, hi, 1)

    sort(0, n - 1)
    return pairs


_SLABS = 16
_NET = _sort_network(_SLABS)


def _largest_desc(slabs, count):
    s = list(slabs) + [None] * (_SLABS - len(slabs))
    for i, j in _NET:
        if s[j] is None:
            continue
        if s[i] is None:
            s[i], s[j] = s[j], None
        else:
            s[i], s[j] = jnp.maximum(s[i], s[j]), jnp.minimum(s[i], s[j])
    depth = len(slabs)
    assert all(x is not None for x in s[:depth]) and all(x is None for x in s[depth:])
    vals = []
    for r in range(count):
        m = jnp.max(s[0], axis=0, keepdims=True)
        vals.append(m)
        hit = s[0] == m
        for l in range(min(depth, count - r - 1)):
            s[l] = jnp.where(hit, s[l + 1] if l + 1 < depth else NEG_INF, s[l])
    return vals


def _slabs(v):
    return [v[8 * k:8 * (k + 1)] for k in range(v.shape[0] // 8)]


def _topk_kernel(s_ref, g_ref, kap_ref):
    K = PEER_TOPK
    s = jnp.concatenate([s_ref[0, 0, c] for c in range(s_ref.shape[2])], axis=1)
    s1 = s[0:N_KEYS]
    s2 = s[N_KEYS:2 * N_KEYS]
    a = _largest_desc(_slabs(s1), K + 1)
    b = _largest_desc(_slabs(s2), K + 1)
    bmat = jnp.concatenate(b[:K], axis=0)
    row8 = lax.broadcasted_iota(jnp.int32, (8, s1.shape[1]), 0)
    cands = _slabs(a[0] + bmat)
    for r in range(1, K // 2):
        cands.append(jnp.where(row8 < (K + 1) // (r + 1), a[r] + bmat[0:8], NEG_INF))
    cands.append(jnp.concatenate(a[K // 2:K], axis=0) + b[0])
    cands.append(jnp.where(row8 == 0, a[0] + b[K], jnp.where(row8 == 1, a[K] + b[0], NEG_INF)))
    top = _largest_desc(cands, K + 1)
    tau = 0.5 * (top[K - 1] + top[K])
    mx = top[0]
    zsum = top[0] * 0.0
    for r in range(K):
        zsum = zsum + jnp.exp2(top[r] - mx)
    shift = mx + jnp.log2(zsum)
    g = s1 - shift
    kap = tau - shift
    for ci in range(s1.shape[1] // LANES):
        cols = slice(ci * LANES, (ci + 1) * LANES)
        g_ref[0, 0, ci] = g[:, cols]
        kap_ref[0, 0, ci] = kap[:, cols]


def _topk(st, tp):
    B, H, nchunks, _, _ = st.shape
    out = jax.ShapeDtypeStruct((B, H, nchunks, N_KEYS, LANES), F32)
    ospec = pl.BlockSpec((1, 1, tp // LANES, N_KEYS, LANES), lambda b, h, t: (b, h, t, 0, 0))
    return pl.pallas_call(
        _topk_kernel,
        grid=(B, H, nchunks * LANES // tp),
        in_specs=[pl.BlockSpec((1, 1, tp // LANES, 2 * N_KEYS, LANES), lambda b, h, t: (b, h, t, 0, 0))],
        out_specs=[ospec,
                   pl.BlockSpec((1, 1, tp // LANES, 1, LANES), lambda b, h, t: (b, h, t, 0, 0))],
        out_shape=[out, jax.ShapeDtypeStruct((B, H, nchunks, 1, LANES), F32)],
        compiler_params=_params(("parallel", "parallel", "parallel")),
        name="topk",
    )(st)


def _expert_tables_kernel(u_ref, v_ref, ub_ref, vt_ref):
    ub_ref[...] = u_ref[...].astype(BF16)
    vt_ref[0] = v_ref[...].T.astype(BF16)


def _expert_tables(u, v, te):
    E, D = u.shape
    rows = pl.BlockSpec((te, D), lambda e: (e, 0))
    return pl.pallas_call(
        _expert_tables_kernel,
        grid=(E // te,),
        in_specs=[rows, rows],
        out_specs=[rows, pl.BlockSpec((1, D, te), lambda e: (e, 0, 0))],
        out_shape=[jax.ShapeDtypeStruct((E, D), BF16), jax.ShapeDtypeStruct((E // te, D, te), BF16)],
        compiler_params=_params(("parallel",)),
        name="expert_tables",
    )(u, v)


def _peer_kernel(hpt_ref, u_ref, un_ref, vt_ref, vp_ref, s2_ref, g_ref, kap_ref, x1_ref, g2_ref,
                 lng_ref, lnb_ref, o_ref, acc_ref, z0_ref, z1_ref, a0_ref, a1_ref, *, alpha, n_steps):
    e = pl.program_id(2)
    n_chunks = acc_ref.shape[0]
    te = u_ref.shape[0]
    n_i = te // N_KEYS
    sub = PEER_CHUNK // LANES
    upper = (e * n_i) % 8 >= n_i

    @pl.when(e == 0)
    def _():
        acc_ref[...] = jnp.zeros_like(acc_ref)
        a1_ref[...] = jnp.zeros_like(a1_ref)
        z0_ref[...] = jnp.dot(u_ref[...], hpt_ref[0, 0], preferred_element_type=F32)

    def chunk(k, z_cur, a_cur, z_next, a_prev):
        def next_scores():
            if k + 1 < n_chunks:
                z_next[...] = jnp.dot(u_ref[...], hpt_ref[0, k + 1], preferred_element_type=F32)
            else:
                z_next[...] = jnp.dot(un_ref[...], hpt_ref[0, 0], preferred_element_type=F32)

        def prev_output():
            if k > 0:
                acc_ref[k - 1] += jnp.dot(vt_ref[0], a_prev[...], preferred_element_type=F32)
            else:
                acc_ref[n_chunks - 1] += jnp.dot(vp_ref[0], a_prev[...], preferred_element_type=F32)

        def tile(ii, cc):
            rows = slice(ii * N_KEYS, (ii + 1) * N_KEYS)
            cols = slice(cc * LANES, (cc + 1) * LANES)
            c = k * sub + cc
            w = jnp.zeros((N_KEYS, LANES), F32)
            for h in range(PEER_HEADS):
                grow = g_ref[0, h, c, ii:ii + 1, :]
                if 2 * n_i <= 8:
                    grow = jnp.where(upper, g_ref[0, h, c, n_i + ii:n_i + ii + 1, :], grow)
                t = s2_ref[0, h, c] + grow
                w = w + jnp.exp2(jnp.where(t >= kap_ref[0, h, c], t, NEG_INF))
            zt = z_cur[rows, cols]
            act = zt * (1.0 + lax.erf(zt * (2.0 ** -0.5)))
            a_cur[rows, cols] = (w * act).astype(BF16)

        tiles = [(ii, cc) for ii in range(n_i) for cc in range(sub)]
        prev_output()
        for ii, cc in tiles[:len(tiles) // 2]:
            tile(ii, cc)
        next_scores()
        for ii, cc in tiles[len(tiles) // 2:]:
            tile(ii, cc)

    for k in range(n_chunks):
        if k % 2 == 0:
            chunk(k, z0_ref, a0_ref, z1_ref, a1_ref)
        else:
            chunk(k, z1_ref, a1_ref, z0_ref, a0_ref)

    @pl.when(e == n_steps - 1)
    def _():
        acc_ref[n_chunks - 1] += jnp.dot(vt_ref[0], a1_ref[...], preferred_element_type=F32)
        half_gate = 0.5 * g2_ref[0]
        for k in range(n_chunks):
            rows = slice(k * PEER_CHUNK, (k + 1) * PEER_CHUNK)
            y = acc_ref[k].T
            o_ref[0, rows, :] = _layernorm(alpha * x1_ref[0, rows, :] + half_gate * y,
                                           lng_ref[...], lnb_ref[...])


def _peer(hpt, u, vt, s2, g, kap, x1, mod3, ln_g, ln_b, alpha, tm):
    B, _, D, _ = hpt.shape
    T = x1.shape[1]
    n_steps, _, te = vt.shape
    nc = tm // LANES
    nk = tm // PEER_CHUNK
    side = pl.BlockSpec((1, PEER_HEADS, nc, N_KEYS, LANES), lambda b, t, e: (b, 0, t, 1, 0))
    n_i = te // N_KEYS
    assert n_i in (4, 8) and nk % 2 == 0
    keyrows = pl.BlockSpec((1, PEER_HEADS, nc, 8, LANES), lambda b, t, e: (b, 0, t, (e * n_i) // 8, 0))
    return pl.pallas_call(
        functools.partial(_peer_kernel, alpha=alpha, n_steps=n_steps),
        grid=(B, T // tm, n_steps),
        in_specs=[pl.BlockSpec((1, nk, D, PEER_CHUNK), lambda b, t, e: (b, t, 0, 0)),
                  pl.BlockSpec((te, D), lambda b, t, e: (e, 0)),
                  pl.BlockSpec((te, D), lambda b, t, e: (jnp.minimum(e + 1, n_steps - 1), 0)),
                  pl.BlockSpec((1, D, te), lambda b, t, e: (e, 0, 0)),
                  pl.BlockSpec((1, D, te), lambda b, t, e: (jnp.maximum(e - 1, 0), 0, 0)),
                  side, keyrows,
                  pl.BlockSpec((1, PEER_HEADS, nc, 1, LANES), lambda b, t, e: (b, 0, t, 0, 0)),
                  pl.BlockSpec((1, tm, D), lambda b, t, e: (b, t, 0)),
                  pl.BlockSpec((1, 1, D), lambda b, t, e: (b * N_MOD + 5, 0, 0)),
                  pl.BlockSpec((1, D), lambda b, t, e: (0, 0)),
                  pl.BlockSpec((1, D), lambda b, t, e: (0, 0))],
        out_specs=pl.BlockSpec((1, tm, D), lambda b, t, e: (b, t, 0)),
        out_shape=jax.ShapeDtypeStruct((B, T, D), F32),
        scratch_shapes=[pltpu.VMEM((nk, D, PEER_CHUNK), F32),
                        pltpu.VMEM((te, PEER_CHUNK), F32), pltpu.VMEM((te, PEER_CHUNK), F32),
                        pltpu.VMEM((te, PEER_CHUNK), BF16), pltpu.VMEM((te, PEER_CHUNK), BF16)],
        compiler_params=_params(("parallel", "parallel", "arbitrary"), PEER_VMEM_LIMIT),
        name="peer",
    )(hpt, u, u, vt, vt, s2, g, kap, x1, mod3, ln_g, ln_b)


def _rope_tables(n_tokens):
    rows = n_tokens // GRID_W
    row = jnp.repeat(jnp.arange(rows, dtype=F32), GRID_W)
    col = jnp.tile(jnp.arange(GRID_W, dtype=F32), rows)
    n_freq = ATT_DH // 4
    inv = ROPE_THETA ** (-jnp.arange(n_freq, dtype=F32) / n_freq)
    ar, ac = row[:, None] * inv, col[:, None] * inv
    cos = jnp.concatenate([jnp.cos(ar), jnp.cos(ar), jnp.cos(ac), jnp.cos(ac)], axis=1)
    sin = jnp.concatenate([-jnp.sin(ar), jnp.sin(ar), -jnp.sin(ac), jnp.sin(ac)], axis=1)
    return jnp.tile(cos, (1, 2)), jnp.tile(sin, (1, 2))


def _dup_heads(w, n_heads, dh):
    return jnp.concatenate([w[:, h * dh:(h + 1) * dh] for h in range(n_heads) for _ in range(2)], axis=1)


def kernel(x, c, ctx, c_ctx, w_ada, b_ada, w_in, w_gate2_f, b_gate_f, w_gate2_b, b_gate_b, gla_norm_g,
           attn_sink, w_out, ln1_g, ln1_b, peer_wq, peer_subkeys, peer_u, peer_v, ln2_g, ln2_b):
    B, T, D = x.shape
    L = ctx.shape[1]
    depth = w_ada.shape[0]
    assert depth == 1, "single-layer block only"
    alpha = (2.0 * depth) ** 0.25

    cond = jnp.concatenate([jax.nn.silu(c), jax.nn.silu(c_ctx)[None]], axis=0)
    cond_bc = jnp.broadcast_to(cond[:, :, None], (B + 1, D, LANES))
    mod = _ada(cond_bc, w_ada[0], b_ada[0][None])
    mod3 = mod.reshape(8 * N_MOD, 1, D)

    wi = w_in[0]
    kq = 2 * GLA_HEADS * GLA_DK + 2 * GLA_HEADS * GLA_DV
    z0, a0 = kq, kq + 2 * GLA_RANK
    k0 = a0 + ATT_HEADS * ATT_DH
    v0 = k0 + ATT_KV_HEADS * ATT_DH
    w_p = jnp.concatenate([
        wi[:, :kq], wi[:, a0:k0],
        _dup_heads(wi[:, k0:v0], ATT_KV_HEADS, ATT_DH),
        wi[:, v0:],
        wi[:, z0:a0], jnp.zeros((D, LANES - 2 * GLA_RANK), F32)], axis=1).astype(BF16)
    gw = GLA_HEADS * GLA_DK
    w2 = jnp.zeros((LANES, 2 * gw), F32)
    w2 = w2.at[0:GLA_RANK, 0:gw].set(w_gate2_f[0]).at[GLA_RANK:2 * GLA_RANK, gw:].set(w_gate2_b[0]).astype(BF16)
    bg = jnp.concatenate([b_gate_f[0], b_gate_b[0]])[None]
    cos, sin = _rope_tables(T)
    ones = jnp.ones((L, LANES), F32)

    gq, gk, gv, r, g, aq, ak, av = _inproj(x, mod3, lambda b: b * N_MOD + 1, lambda b: b * N_MOD,
                                           w_p, w2, bg, cos, sin, tm=min(T, 512))
    _, cgk, cgv, _, cg, _, cak, cav = _inproj(ctx, mod3, lambda b: B * N_MOD + 1, lambda b: B * N_MOD,
                                              w_p, w2, bg, ones, 0.0 * ones, tm=L)

    of, ob = _gla(gq, gk, g, gv, cgk, cg, cgv, tt=min(T, 512))
    sink_bc = jnp.broadcast_to(attn_sink[0][:, None], (ATT_HEADS, LANES))
    att = _attn(aq, ak, av, cak, cav, sink_bc)

    sk = peer_subkeys[0]
    half = PEER_DQ // 2
    skb = jnp.zeros((PEER_HEADS, 2 * N_KEYS, PEER_DQ), F32)
    skb = skb.at[:, :N_KEYS, :half].set(sk[:, 0]).at[:, N_KEYS:, half:].set(sk[:, 1])
    skb = skb.astype(BF16)
    x1, hpt, st = _outproj(of, ob, r, att, x, mod3, w_out[0].astype(BF16), gla_norm_g[0][None],
                           ln1_g[0][None], ln1_b[0][None], peer_wq[0].T.astype(BF16), skb,
                           alpha, tm=min(T, 512))

    g, kap = _topk(st, tp=min(T, 512))
    u, vt = _expert_tables(peer_u[0], peer_v[0], te=1024)
    return _peer(hpt, u, vt, st, g, kap, x1, mod3, ln2_g[0][None], ln2_b[0][None],
                 alpha, tm=min(T, 1024))
```

```python
import functools

import jax
import jax.numpy as jnp
from jax import lax
from jax.experimental import pallas as pl
from jax.experimental.pallas import tpu as pltpu

F32 = jnp.float32
BF16 = jnp.bfloat16

GRID_W = 64
GLA_HEADS = 4
GLA_DK = 64
GLA_DV = 128
GLA_RANK = 16
GLA_NORMALIZER = 16.0
GLA_CHUNK = 64
ATT_HEADS = 8
ATT_KV_HEADS = 2
ATT_DH = 64
WINDOW = 128
ATT_BLOCK = 128
ROPE_THETA = 10000.0
PEER_HEADS = 8
N_KEYS = 128
PEER_DQ = 128
PEER_TOPK = 16
N_MOD = 6
LN_EPS = 1e-5
NEG_INF = -1e30
LANES = 128
PEER_CHUNK = 256
LOG2E = 1.4426950408889634

ADA_COLS = 1024
INPROJ_TOKENS = 512
GLA_TOKENS = 512
OUTPROJ_TOKENS = 512
TOPK_TOKENS = 1024
PEER_TOKENS = 1024
PEER_EXPERTS = 1024

_GW = GLA_HEADS * GLA_DK
_SEGS = (_GW, _GW, GLA_HEADS * GLA_DV, GLA_HEADS * GLA_DV, ATT_HEADS * ATT_DH,
         2 * ATT_KV_HEADS * ATT_DH, ATT_KV_HEADS * ATT_DH, LANES)
_OFFS = tuple(sum(_SEGS[:i]) for i in range(len(_SEGS) + 1))
IN_COLS = _OFFS[-1]
assert all(o % LANES == 0 for o in _OFFS)

VMEM_LIMIT = 56 * 1024 * 1024
PEER_VMEM_LIMIT = 60 * 1024 * 1024

_NT = (((1,), (1,)), ((), ()))
_TN = (((0,), (0,)), ((), ()))


def _params(sem, vmem_limit=VMEM_LIMIT):
    return pltpu.CompilerParams(dimension_semantics=sem, vmem_limit_bytes=vmem_limit)


def _ada_kernel(s_ref, w_ref, b_ref, o_ref, *, n_rows):
    tn = w_ref.shape[1]
    w = w_ref[...]
    rows = []
    for r in range(n_rows):
        s = s_ref[r]
        cols = [jnp.sum(w[:, j * LANES:(j + 1) * LANES] * s, axis=0, keepdims=True)
                for j in range(tn // LANES)]
        rows.append(jnp.concatenate(cols, axis=1) + b_ref[...])
    rows.append(jnp.zeros((8 - n_rows, tn), F32))
    o_ref[...] = jnp.concatenate(rows, axis=0)


def _ada(cond_bc, w, b):
    n_rows, d, _ = cond_bc.shape
    n = w.shape[1]
    tn = ADA_COLS
    return pl.pallas_call(
        functools.partial(_ada_kernel, n_rows=n_rows),
        grid=(n // tn,),
        in_specs=[pl.BlockSpec((n_rows, d, LANES), lambda j: (0, 0, 0)),
                  pl.BlockSpec((d, tn), lambda j: (0, j)),
                  pl.BlockSpec((1, tn), lambda j: (0, j))],
        out_specs=pl.BlockSpec((8, tn), lambda j: (0, j)),
        out_shape=jax.ShapeDtypeStruct((8, n), F32),
        compiler_params=_params(("parallel",)),
        name="ada",
    )(cond_bc, w, b)


def _rope(a, cos, sin):
    n = a.shape[1] // LANES
    c = jnp.concatenate([cos] * n, axis=1)
    s = jnp.concatenate([sin] * n, axis=1)
    lane = lax.broadcasted_iota(jnp.int32, a.shape, 1)
    swapped = jnp.where((lane % 32) < 16,
                        pltpu.roll(a, a.shape[1] - 16, 1),
                        pltpu.roll(a, 16, 1))
    return a * c + swapped * s


def _inproj_kernel(x_ref, sc_ref, sh_ref, w_ref, w2_ref, bg_ref, cos_ref, sin_ref,
                   gq_ref, gk_ref, gv_ref, r_ref, g_ref, aq_ref, ak_ref, av_ref):
    h = x_ref[0] * (1.0 + sc_ref[0]) + sh_ref[0]
    y = jnp.dot(h.astype(BF16), w_ref[...], preferred_element_type=F32)
    seg = lambda i: y[:, _OFFS[i]:_OFFS[i + 1]]
    gq_ref[0] = seg(0) * (GLA_DK ** -0.5)
    gk_ref[0] = seg(1)
    gv_ref[0] = seg(2).astype(BF16)
    r_ref[0] = seg(3)
    cos = cos_ref[...]
    sin = sin_ref[...]
    aq_ref[0] = (_rope(seg(4), cos, sin) * (ATT_DH ** -0.5 * LOG2E)).astype(BF16)
    ak_ref[0] = _rope(seg(5), cos, sin).astype(BF16)
    av_ref[0] = seg(6).astype(BF16)
    z = seg(7)
    logits = jnp.dot(z.astype(BF16), w2_ref[...], preferred_element_type=F32) + bg_ref[...]
    g = (jnp.minimum(logits, 0.0) - jnp.log1p(jnp.exp(-jnp.abs(logits)))) * (1.0 / GLA_NORMALIZER)
    g_ref[0, 0] = g[:, :_GW]
    g_ref[1, 0] = g[:, _GW:]


def _inproj(x, mod3, sc_row, sh_row, w_p, w2, bg, cos, sin, tm):
    B, T, D = x.shape
    NP = w_p.shape[1]
    tok = lambda w, dt: jax.ShapeDtypeStruct((B, T, w), dt)
    tspec = lambda w: pl.BlockSpec((1, tm, w), lambda b, t: (b, t, 0))
    return pl.pallas_call(
        _inproj_kernel,
        grid=(B, T // tm),
        in_specs=[tspec(D),
                  pl.BlockSpec((1, 1, D), lambda b, t: (sc_row(b), 0, 0)),
                  pl.BlockSpec((1, 1, D), lambda b, t: (sh_row(b), 0, 0)),
                  pl.BlockSpec((D, NP), lambda b, t: (0, 0)),
                  pl.BlockSpec((LANES, 2 * _GW), lambda b, t: (0, 0)),
                  pl.BlockSpec((1, 2 * _GW), lambda b, t: (0, 0)),
                  pl.BlockSpec((tm, LANES), lambda b, t: (t, 0)),
                  pl.BlockSpec((tm, LANES), lambda b, t: (t, 0))],
        out_specs=[tspec(_SEGS[0]), tspec(_SEGS[1]), tspec(_SEGS[2]), tspec(_SEGS[3]),
                   pl.BlockSpec((2, 1, tm, _GW), lambda b, t: (0, b, t, 0)),
                   tspec(_SEGS[4]), tspec(_SEGS[5]), tspec(_SEGS[6])],
        out_shape=[tok(_SEGS[0], F32), tok(_SEGS[1], F32), tok(_SEGS[2], BF16), tok(_SEGS[3], F32),
                   jax.ShapeDtypeStruct((2, B, T, _GW), F32),
                   tok(_SEGS[4], BF16), tok(_SEGS[5], BF16), tok(_SEGS[6], BF16)],
        compiler_params=_params(("parallel", "parallel")),
        name="inproj",
    )(x, mod3, mod3, w_p, w2, bg, cos, sin)


def _gla_kernel(qf_ref, kf_ref, gf_ref, vf_ref, qb_ref, kb_ref, gb_ref, vb_ref, ck_ref, cg_ref, cv_ref,
                of_ref, ob_ref, st_ref, *, n_chunks, n_ctx_chunks):
    t = pl.program_id(0)
    nb = qf_ref.shape[0]
    C = GLA_CHUNK
    npairs = GLA_HEADS // 2
    ri = lax.broadcasted_iota(jnp.int32, (C, C), 0)
    ci = lax.broadcasted_iota(jnp.int32, (C, C), 1)
    tri = [ri >= ci, ri <= ci]
    tri_bf = [m.astype(BF16) for m in tri]
    tri2 = [jnp.concatenate([m, m], axis=0) for m in tri]
    lane = lax.broadcasted_iota(jnp.int32, (1, LANES), 1)
    m0 = (lane < GLA_DK).astype(F32)
    m1 = 1.0 - m0
    rr = lax.broadcasted_iota(jnp.int32, (2 * GLA_DV, LANES), 0)
    ll = lax.broadcasted_iota(jnp.int32, (2 * GLA_DV, LANES), 1)
    blockdiag = (rr < GLA_DV) == (ll < GLA_DK)

    def chunks(items, with_out):
        bs = []
        for (_, d, _), _, gc, _, _ in items:
            ghi = gc.astype(BF16)
            glo = (gc - ghi.astype(F32)).astype(BF16)
            bs.append(jnp.dot(tri_bf[d], ghi, preferred_element_type=F32)
                      + jnp.dot(tri_bf[d], glo, preferred_element_type=F32))
        btots = [jnp.sum(it[2], axis=0, keepdims=True) for it in items]
        sts = [st_ref[it[0]] for it in items]
        outs = [None] * len(items)
        if with_out:
            qes = [it[4] * jnp.exp(b) for it, b in zip(items, bs)]
            amats = []
            for it, b, qe in zip(items, bs, qes):
                ke = (it[1] * jnp.exp(-b)).astype(BF16)
                lhs = jnp.concatenate([qe * m0, qe * m1], axis=0).astype(BF16)
                amats.append(lax.dot_general(lhs, ke, _NT, preferred_element_type=F32))
            inters = [lax.dot_general(qe.astype(BF16), st.astype(BF16), _NT, preferred_element_type=F32)
                      for qe, st in zip(qes, sts)]
            for n, (it, a) in enumerate(zip(items, amats)):
                a = jnp.where(tri2[it[0][1]], a, 0.0).astype(BF16)
                oi = jnp.dot(a, it[3], preferred_element_type=F32)
                outs[n] = jnp.concatenate([oi[:C, :GLA_DV], oi[C:, GLA_DV:]], axis=1) + inters[n]
        for it, b, btot, st in zip(items, bs, btots, sts):
            kd = (it[1] * jnp.exp(btot - b)).astype(BF16)
            mt = lax.dot_general(it[3], kd, _TN, preferred_element_type=F32)
            st_ref[it[0]] = st * jnp.exp(btot) + jnp.where(blockdiag, mt, 0.0)
        return outs

    def qk_lanes(p):
        return slice(p * LANES, (p + 1) * LANES)

    def v_lanes(p):
        return slice(p * 2 * GLA_DV, (p + 1) * 2 * GLA_DV)

    @pl.when(t == 0)
    def _():
        st_ref[...] = jnp.zeros_like(st_ref)
        for j in range(n_ctx_chunks):
            items = []
            for bi in range(nb):
                for d in range(2):
                    cj = j if d == 0 else n_ctx_chunks - 1 - j
                    rows = slice(cj * C, (cj + 1) * C)
                    items += [((bi, d, p), ck_ref[bi, rows, qk_lanes(p)], cg_ref[d, bi, rows, qk_lanes(p)],
                               cv_ref[bi, rows, v_lanes(p)], None) for p in range(npairs)]
            chunks(items, False)

    streams = [(qf_ref, kf_ref, gf_ref, vf_ref, of_ref), (qb_ref, kb_ref, gb_ref, vb_ref, ob_ref)]

    def body(j, carry):
        items, dests = [], []
        for bi in range(nb):
            for d, (q_ref, k_ref, g_ref, v_ref, o_ref) in enumerate(streams):
                cj = j if d == 0 else n_chunks - 1 - j
                rows = pl.ds(pl.multiple_of(cj * C, C), C)
                for p in range(npairs):
                    items.append(((bi, d, p), k_ref[bi, rows, qk_lanes(p)], g_ref[0, bi, rows, qk_lanes(p)],
                                  v_ref[bi, rows, v_lanes(p)], q_ref[bi, rows, qk_lanes(p)]))
                    dests.append((o_ref, bi, rows, v_lanes(p)))
        for (o_ref, bi, rows, lanes), out in zip(dests, chunks(items, True)):
            o_ref[bi, rows, lanes] = out
        return carry

    lax.fori_loop(0, n_chunks, body, 0)


def _gla(gq, gk, g, gv, ck, cg, cv, tt):
    B, T, W = gq.shape
    L = ck.shape[1]
    nT = T // tt
    VW = gv.shape[2]
    fwd = lambda t: (0, t, 0)
    bwd = lambda t: (0, nT - 1 - t, 0)
    ctx = lambda t: (0, 0, 0)
    out = jax.ShapeDtypeStruct((B, T, VW), F32)
    return pl.pallas_call(
        functools.partial(_gla_kernel, n_chunks=tt // GLA_CHUNK, n_ctx_chunks=L // GLA_CHUNK),
        grid=(nT,),
        in_specs=[pl.BlockSpec((B, tt, W), fwd), pl.BlockSpec((B, tt, W), fwd),
                  pl.BlockSpec((1, B, tt, W), lambda t: (0, 0, t, 0)),
                  pl.BlockSpec((B, tt, VW), fwd),
                  pl.BlockSpec((B, tt, W), bwd), pl.BlockSpec((B, tt, W), bwd),
                  pl.BlockSpec((1, B, tt, W), lambda t: (1, 0, nT - 1 - t, 0)),
                  pl.BlockSpec((B, tt, VW), bwd),
                  pl.BlockSpec((B, L, W), ctx),
                  pl.BlockSpec((2, B, L, W), lambda t: (0, 0, 0, 0)),
                  pl.BlockSpec((B, L, VW), ctx)],
        out_specs=[pl.BlockSpec((B, tt, VW), fwd), pl.BlockSpec((B, tt, VW), bwd)],
        out_shape=[out, out],
        scratch_shapes=[pltpu.VMEM((B, 2, GLA_HEADS // 2, 2 * GLA_DV, LANES), F32)],
        compiler_params=_params(("arbitrary",)),
        name="gla",
    )(gq, gk, g, gv, gq, gk, g, gv, ck, cg, cv)


def _attn_kernel(q_ref, kp_ref, kc_ref, kn_ref, vp_ref, vc_ref, vn_ref, ck_ref, cv_ref, sink_ref,
                 o_ref, *, n_blocks):
    n = pl.program_id(1)
    BLK = ATT_BLOCK
    L = ck_ref.shape[1]
    nk = 3 * BLK + L
    group = ATT_HEADS // ATT_KV_HEADS
    ki = lax.broadcasted_iota(jnp.int32, (nk, group * BLK), 0)
    qi = lax.broadcasted_iota(jnp.int32, (nk, group * BLK), 1) & (BLK - 1)
    band = jnp.abs(ki - BLK - qi) <= WINDOW
    prev_ok = jnp.logical_or(ki >= BLK, n > 0)
    next_ok = jnp.logical_or(ki < 2 * BLK, n < n_blocks - 1)
    valid = jnp.logical_or(ki >= 3 * BLK, band & prev_ok & next_ok)
    lane = lax.broadcasted_iota(jnp.int32, (BLK, LANES), 1)
    zero = jnp.zeros((), BF16)
    items = range(ATT_HEADS // 2)
    width = 2 * BLK

    def k_rows(gi):
        kv = 2 * gi // group
        ks = slice(kv * LANES, (kv + 1) * LANES)
        return jnp.concatenate([r[0, :, ks] for r in (kp_ref, kc_ref, kn_ref, ck_ref)], axis=0)

    v_rows = jnp.concatenate([r[0] for r in (vp_ref, vc_ref, vn_ref, cv_ref)], axis=0)

    ok = valid[:, :width]
    scores, snks = [], []
    for gi in items:
        q2 = q_ref[0, :, gi * LANES:(gi + 1) * LANES]
        qstack = jnp.concatenate([jnp.where(lane < ATT_DH, q2, zero), jnp.where(lane >= ATT_DH, q2, zero)],
                                 axis=0)
        s = lax.dot_general(k_rows(gi), qstack, _NT,
                            preferred_element_type=F32)
        scores.append(jnp.where(ok, s, NEG_INF))
        snks.append(jnp.concatenate([sink_ref[2 * gi + hh:2 * gi + hh + 1, :] for hh in range(2)], axis=1))
    maxes = [jnp.maximum(jnp.max(s, axis=0, keepdims=True), snk) for s, snk in zip(scores, snks)]
    probs = [jnp.exp2(s - m) for s, m in zip(scores, maxes)]
    dens = [jnp.sum(p, axis=0, keepdims=True) + jnp.exp2(snk - m) for p, snk, m in zip(probs, snks, maxes)]
    outs = [lax.dot_general(v_rows, p.astype(BF16), _TN, preferred_element_type=F32)
            for p in probs]
    for gi, ot, den in zip(items, outs, dens):
        kv = 2 * gi // group
        ot = ot[kv * ATT_DH:(kv + 1) * ATT_DH] / den
        pair = jnp.concatenate([ot[:, :BLK], ot[:, BLK:]], axis=0)
        o_ref[0, :, gi * LANES:(gi + 1) * LANES] = pair.T.astype(o_ref.dtype)


def _attn(aq, ak, av, ck, cv, sink_bc):
    B, T, _ = aq.shape
    L = ck.shape[1]
    nb = T // ATT_BLOCK
    kw, vw = ak.shape[2], av.shape[2]
    cur = lambda b, n: (b, n, 0)
    prev = lambda b, n: (b, jnp.maximum(n - 1, 0), 0)
    nxt = lambda b, n: (b, jnp.minimum(n + 1, nb - 1), 0)
    kspec = lambda im: pl.BlockSpec((1, ATT_BLOCK, kw), im)
    vspec = lambda im: pl.BlockSpec((1, ATT_BLOCK, vw), im)
    return pl.pallas_call(
        functools.partial(_attn_kernel, n_blocks=nb),
        grid=(B, nb),
        in_specs=[pl.BlockSpec((1, ATT_BLOCK, ATT_HEADS * ATT_DH), cur),
                  kspec(prev), kspec(cur), kspec(nxt), vspec(prev), vspec(cur), vspec(nxt),
                  pl.BlockSpec((1, L, kw), lambda b, n: (b, 0, 0)),
                  pl.BlockSpec((1, L, vw), lambda b, n: (b, 0, 0)),
                  pl.BlockSpec((ATT_HEADS, LANES), lambda b, n: (0, 0))],
        out_specs=pl.BlockSpec((1, ATT_BLOCK, ATT_HEADS * ATT_DH), cur),
        out_shape=jax.ShapeDtypeStruct((B, T, ATT_HEADS * ATT_DH), BF16),
        compiler_params=_params(("parallel", "parallel")),
        name="attn",
    )(aq, ak, ak, ak, av, av, av, ck, cv, sink_bc)


def _layernorm(v, g, b):
    mu = jnp.mean(v, axis=-1, keepdims=True)
    vc = v - mu
    var = jnp.mean(vc * vc, axis=-1, keepdims=True)
    return vc * lax.rsqrt(var + LN_EPS) * g + b


def _outproj_kernel(of_ref, ob_ref, r_ref, att_ref, x_ref, g1_ref, sh2_ref, sc2_ref,
                    wout_ref, gng_ref, lng_ref, lnb_ref, wqt_ref, sk_ref,
                    x1_ref, hpt_ref, st_ref, *, alpha):
    o = of_ref[0] + ob_ref[0]
    r = r_ref[0]
    gng = gng_ref[...]
    parts = []
    for h in range(GLA_HEADS):
        oh = o[:, h * GLA_DV:(h + 1) * GLA_DV]
        oh = oh * lax.rsqrt(jnp.mean(oh * oh, axis=-1, keepdims=True) + LN_EPS) * gng
        rh = r[:, h * GLA_DV:(h + 1) * GLA_DV]
        parts.append((oh * (rh * jax.nn.sigmoid(rh))).astype(BF16))
    parts.append(att_ref[0])
    cat = jnp.concatenate(parts, axis=1)
    y = jnp.dot(cat, wout_ref[...], preferred_element_type=F32)
    x1 = _layernorm(alpha * x_ref[0] + g1_ref[0] * y, lng_ref[...], lnb_ref[...])
    x1_ref[0] = x1
    hp = x1 * (1.0 + sc2_ref[0]) + sh2_ref[0]
    hpt = hp.T.astype(BF16)
    for c in range(hpt.shape[1] // PEER_CHUNK):
        hpt_ref[0, c] = hpt[:, c * PEER_CHUNK:(c + 1) * PEER_CHUNK]
    qt = jnp.dot(wqt_ref[...], hpt, preferred_element_type=F32)
    for h in range(PEER_HEADS):
        qh = qt[h * PEER_DQ:(h + 1) * PEER_DQ, :].astype(BF16)
        s = jnp.dot(sk_ref[h], qh, preferred_element_type=F32) * LOG2E
        for c in range(s.shape[1] // LANES):
            st_ref[0, h, c] = s[:, c * LANES:(c + 1) * LANES]


def _outproj(of, ob, r, att, x, mod3, w_out, gng, ln_g, ln_b, wqt, skb, alpha, tm):
    B, T, D = x.shape
    W = of.shape[2]
    tok = lambda w: pl.BlockSpec((1, tm, w), lambda b, t: (b, t, 0))
    modrow = lambda j: pl.BlockSpec((1, 1, D), lambda b, t: (b * N_MOD + j, 0, 0))
    full = lambda a: pl.BlockSpec(a.shape, lambda b, t: (0,) * a.ndim)
    return pl.pallas_call(
        functools.partial(_outproj_kernel, alpha=alpha),
        grid=(B, T // tm),
        in_specs=[tok(W), tok(W), tok(W), tok(att.shape[2]), tok(D), modrow(2), modrow(3), modrow(4),
                  full(w_out), full(gng), full(ln_g), full(ln_b), full(wqt), full(skb)],
        out_specs=[tok(D),
                   pl.BlockSpec((1, tm // PEER_CHUNK, D, PEER_CHUNK), lambda b, t: (b, t, 0, 0)),
                   pl.BlockSpec((1, PEER_HEADS, tm // LANES, 2 * N_KEYS, LANES), lambda b, t: (b, 0, t, 0, 0))],
        out_shape=[jax.ShapeDtypeStruct((B, T, D), F32),
                   jax.ShapeDtypeStruct((B, T // PEER_CHUNK, D, PEER_CHUNK), BF16),
                   jax.ShapeDtypeStruct((B, PEER_HEADS, T // LANES, 2 * N_KEYS, LANES), F32)],
        compiler_params=_params(("parallel", "parallel")),
        name="outproj",
    )(of, ob, r, att, x, mod3, mod3, mod3, w_out, gng, ln_g, ln_b, wqt, skb)


def _sort_network(n):
    pairs = []

    def merge(lo, hi, r):
        step = 2 * r
        if step < hi - lo:
            merge(lo, hi, step)
            merge(lo + r, hi, step)
            pairs.extend((i, i + r) for i in range(lo + r, hi - r, step))
        else:
            pairs.append((lo, lo + r))

    def sort(lo, hi):
        if hi - lo >= 1:
            mid = lo + (hi - lo) // 2
            sort(lo, mid)
            sort(mid + 1, hi)
            merge(lo, hi, 1)

    sort(0, n - 1)
    return pairs


_SLABS = 16
_NET = _sort_network(_SLABS)


def _largest_desc(slabs, count):
    s = list(slabs) + [None] * (_SLABS - len(slabs))
    for i, j in _NET:
        if s[j] is None:
            continue
        if s[i] is None:
            s[i], s[j] = s[j], None
        else:
            s[i], s[j] = jnp.maximum(s[i], s[j]), jnp.minimum(s[i], s[j])
    depth = len(slabs)
    assert all(x is not None for x in s[:depth]) and all(x is None for x in s[depth:])
    vals = []
    for r in range(count):
        m = jnp.max(s[0], axis=0, keepdims=True)
        vals.append(m)
        hit = s[0] == m
        for l in range(min(depth, count - r - 1)):
            s[l] = jnp.where(hit, s[l + 1] if l + 1 < depth else NEG_INF, s[l])
    return vals


def _slabs(v):
    return [v[8 * k:8 * (k + 1)] for k in range(v.shape[0] // 8)]


def _topk_kernel(s_ref, g_ref, kap_ref):
    K = PEER_TOPK
    s = jnp.concatenate([s_ref[0, 0, c] for c in range(s_ref.shape[2])], axis=1)
    s1 = s[0:N_KEYS]
    s2 = s[N_KEYS:2 * N_KEYS]
    a = _largest_desc(_slabs(s1), K + 1)
    b = _largest_desc(_slabs(s2), K + 1)
    bmat = jnp.concatenate(b[:K], axis=0)
    row8 = lax.broadcasted_iota(jnp.int32, (8, s1.shape[1]), 0)
    cands = _slabs(a[0] + bmat)
    for r in range(1, K // 2):
        cands.append(jnp.where(row8 < (K + 1) // (r + 1), a[r] + bmat[0:8], NEG_INF))
    cands.append(jnp.concatenate(a[K // 2:K], axis=0) + b[0])
    cands.append(jnp.where(row8 == 0, a[0] + b[K], jnp.where(row8 == 1, a[K] + b[0], NEG_INF)))
    top = _largest_desc(cands, K + 1)
    tau = 0.5 * (top[K - 1] + top[K])
    mx = top[0]
    zsum = top[0] * 0.0
    for r in range(K):
        zsum = zsum + jnp.exp2(top[r] - mx)
    shift = mx + jnp.log2(zsum)
    g = s1 - shift
    kap = tau - shift
    for ci in range(s1.shape[1] // LANES):
        cols = slice(ci * LANES, (ci + 1) * LANES)
        g_ref[0, 0, ci] = g[:, cols]
        kap_ref[0, 0, ci] = kap[:, cols]


def _topk(st, tp):
    B, H, nchunks, _, _ = st.shape
    out = jax.ShapeDtypeStruct((B, H, nchunks, N_KEYS, LANES), F32)
    ospec = pl.BlockSpec((1, 1, tp // LANES, N_KEYS, LANES), lambda b, h, t: (b, h, t, 0, 0))
    return pl.pallas_call(
        _topk_kernel,
        grid=(B, H, nchunks * LANES // tp),
        in_specs=[pl.BlockSpec((1, 1, tp // LANES, 2 * N_KEYS, LANES), lambda b, h, t: (b, h, t, 0, 0))],
        out_specs=[ospec,
                   pl.BlockSpec((1, 1, tp // LANES, 1, LANES), lambda b, h, t: (b, h, t, 0, 0))],
        out_shape=[out, jax.ShapeDtypeStruct((B, H, nchunks, 1, LANES), F32)],
        compiler_params=_params(("parallel", "parallel", "parallel")),
        name="topk",
    )(st)


def _expert_tables_kernel(u_ref, v_ref, ub_ref, vt_ref):
    ub_ref[...] = u_ref[...].astype(BF16)
    vt_ref[0] = v_ref[...].T.astype(BF16)


def _expert_tables(u, v, te):
    E, D = u.shape
    rows = pl.BlockSpec((te, D), lambda e: (e, 0))
    return pl.pallas_call(
        _expert_tables_kernel,
        grid=(E // te,),
        in_specs=[rows, rows],
        out_specs=[rows, pl.BlockSpec((1, D, te), lambda e: (e, 0, 0))],
        out_shape=[jax.ShapeDtypeStruct((E, D), BF16), jax.ShapeDtypeStruct((E // te, D, te), BF16)],
        compiler_params=_params(("parallel",)),
        name="expert_tables",
    )(u, v)


def _peer_kernel(hpt_ref, u_ref, un_ref, vt_ref, vp_ref, s2_ref, g_ref, kap_ref, x1_ref, g2_ref,
                 lng_ref, lnb_ref, o_ref, acc_ref, z0_ref, z1_ref, a0_ref, a1_ref, *, alpha, n_steps):
    e = pl.program_id(2)
    n_chunks = acc_ref.shape[0]
    te = u_ref.shape[0]
    n_i = te // N_KEYS
    sub = PEER_CHUNK // LANES
    upper = (e * n_i) % 8 >= n_i

    @pl.when(e == 0)
    def _():
        acc_ref[...] = jnp.zeros_like(acc_ref)
        a1_ref[...] = jnp.zeros_like(a1_ref)
        z0_ref[...] = jnp.dot(u_ref[...], hpt_ref[0, 0], preferred_element_type=F32)

    def chunk(k, z_cur, a_cur, z_next, a_prev):
        def next_scores():
            if k + 1 < n_chunks:
                z_next[...] = jnp.dot(u_ref[...], hpt_ref[0, k + 1], preferred_element_type=F32)
            else:
                z_next[...] = jnp.dot(un_ref[...], hpt_ref[0, 0], preferred_element_type=F32)

        def prev_output():
            if k > 0:
                acc_ref[k - 1] += jnp.dot(vt_ref[0], a_prev[...], preferred_element_type=F32)
            else:
                acc_ref[n_chunks - 1] += jnp.dot(vp_ref[0], a_prev[...], preferred_element_type=F32)

        def tile(ii, cc):
            rows = slice(ii * N_KEYS, (ii + 1) * N_KEYS)
            cols = slice(cc * LANES, (cc + 1) * LANES)
            c = k * sub + cc
            w = jnp.zeros((N_KEYS, LANES), F32)
            for h in range(PEER_HEADS):
                grow = g_ref[0, h, c, ii:ii + 1, :]
                if 2 * n_i <= 8:
                    grow = jnp.where(upper, g_ref[0, h, c, n_i + ii:n_i + ii + 1, :], grow)
                t = s2_ref[0, h, c] + grow
                w = w + jnp.exp2(jnp.where(t >= kap_ref[0, h, c], t, NEG_INF))
            zt = z_cur[rows, cols]
            act = zt * (1.0 + lax.erf(zt * (2.0 ** -0.5)))
            a_cur[rows, cols] = (w * act).astype(BF16)

        tiles = [(ii, cc) for ii in range(n_i) for cc in range(sub)]
        prev_output()
        for ii, cc in tiles[:len(tiles) // 2]:
            tile(ii, cc)
        next_scores()
        for ii, cc in tiles[len(tiles) // 2:]:
            tile(ii, cc)

    for k in range(n_chunks):
        if k % 2 == 0:
            chunk(k, z0_ref, a0_ref, z1_ref, a1_ref)
        else:
            chunk(k, z1_ref, a1_ref, z0_ref, a0_ref)

    @pl.when(e == n_steps - 1)
    def _():
        acc_ref[n_chunks - 1] += jnp.dot(vt_ref[0], a1_ref[...], preferred_element_type=F32)
        half_gate = 0.5 * g2_ref[0]
        for k in range(n_chunks):
            rows = slice(k * PEER_CHUNK, (k + 1) * PEER_CHUNK)
            y = acc_ref[k].T
            o_ref[0, rows, :] = _layernorm(alpha * x1_ref[0, rows, :] + half_gate * y,
                                           lng_ref[...], lnb_ref[...])


def _peer(hpt, u, vt, s2, g, kap, x1, mod3, ln_g, ln_b, alpha, tm):
    B, _, D, _ = hpt.shape
    T = x1.shape[1]
    n_steps, _, te = vt.shape
    nc = tm // LANES
    nk = tm // PEER_CHUNK
    side = pl.BlockSpec((1, PEER_HEADS, nc, N_KEYS, LANES), lambda b, t, e: (b, 0, t, 1, 0))
    n_i = te // N_KEYS
    assert n_i in (4, 8) and nk % 2 == 0
    keyrows = pl.BlockSpec((1, PEER_HEADS, nc, 8, LANES), lambda b, t, e: (b, 0, t, (e * n_i) // 8, 0))
    return pl.pallas_call(
        functools.partial(_peer_kernel, alpha=alpha, n_steps=n_steps),
        grid=(B, T // tm, n_steps),
        in_specs=[pl.BlockSpec((1, nk, D, PEER_CHUNK), lambda b, t, e: (b, t, 0, 0)),
                  pl.BlockSpec((te, D), lambda b, t, e: (e, 0)),
                  pl.BlockSpec((te, D), lambda b, t, e: (jnp.minimum(e + 1, n_steps - 1), 0)),
                  pl.BlockSpec((1, D, te), lambda b, t, e: (e, 0, 0)),
                  pl.BlockSpec((1, D, te), lambda b, t, e: (jnp.maximum(e - 1, 0), 0, 0)),
                  side, keyrows,
                  pl.BlockSpec((1, PEER_HEADS, nc, 1, LANES), lambda b, t, e: (b, 0, t, 0, 0)),
                  pl.BlockSpec((1, tm, D), lambda b, t, e: (b, t, 0)),
                  pl.BlockSpec((1, 1, D), lambda b, t, e: (b * N_MOD + 5, 0, 0)),
                  pl.BlockSpec((1, D), lambda b, t, e: (0, 0)),
                  pl.BlockSpec((1, D), lambda b, t, e: (0, 0))],
        out_specs=pl.BlockSpec((1, tm, D), lambda b, t, e: (b, t, 0)),
        out_shape=jax.ShapeDtypeStruct((B, T, D), F32),
        scratch_shapes=[pltpu.VMEM((nk, D, PEER_CHUNK), F32),
                        pltpu.VMEM((te, PEER_CHUNK), F32), pltpu.VMEM((te, PEER_CHUNK), F32),
                        pltpu.VMEM((te, PEER_CHUNK), BF16), pltpu.VMEM((te, PEER_CHUNK), BF16)],
        compiler_params=_params(("parallel", "parallel", "arbitrary"), PEER_VMEM_LIMIT),
        name="peer",
    )(hpt, u, u, vt, vt, s2, g, kap, x1, mod3, ln_g, ln_b)


def _rope_tables(n_tokens):
    rows = n_tokens // GRID_W
    row = jnp.repeat(jnp.arange(rows, dtype=F32), GRID_W)
    col = jnp.tile(jnp.arange(GRID_W, dtype=F32), rows)
    n_freq = ATT_DH // 4
    inv = ROPE_THETA ** (-jnp.arange(n_freq, dtype=F32) / n_freq)
    ar, ac = row[:, None] * inv, col[:, None] * inv
    cos = jnp.concatenate([jnp.cos(ar), jnp.cos(ar), jnp.cos(ac), jnp.cos(ac)], axis=1)
    sin = jnp.concatenate([-jnp.sin(ar), jnp.sin(ar), -jnp.sin(ac), jnp.sin(ac)], axis=1)
    return jnp.tile(cos, (1, 2)), jnp.tile(sin, (1, 2))


def _dup_heads(w, n_heads, dh):
    return jnp.concatenate([w[:, h * dh:(h + 1) * dh] for h in range(n_heads) for _ in range(2)], axis=1)


def kernel(x, c, ctx, c_ctx, w_ada, b_ada, w_in, w_gate2_f, b_gate_f, w_gate2_b, b_gate_b, gla_norm_g,
           attn_sink, w_out, ln1_g, ln1_b, peer_wq, peer_subkeys, peer_u, peer_v, ln2_g, ln2_b):
    B, T, D = x.shape
    L = ctx.shape[1]
    depth = w_ada.shape[0]
    assert depth == 1, "single-layer block only"
    alpha = (2.0 * depth) ** 0.25

    cond = jnp.concatenate([jax.nn.silu(c), jax.nn.silu(c_ctx)[None]], axis=0)
    cond_bc = jnp.broadcast_to(cond[:, :, None], (B + 1, D, LANES))
    mod = _ada(cond_bc, w_ada[0], b_ada[0][None])
    mod3 = mod.reshape(8 * N_MOD, 1, D)

    wi = w_in[0]
    kq = 2 * GLA_HEADS * GLA_DK + 2 * GLA_HEADS * GLA_DV
    z0, a0 = kq, kq + 2 * GLA_RANK
    k0 = a0 + ATT_HEADS * ATT_DH
    v0 = k0 + ATT_KV_HEADS * ATT_DH
    w_p = jnp.concatenate([
        wi[:, :kq], wi[:, a0:k0],
        _dup_heads(wi[:, k0:v0], ATT_KV_HEADS, ATT_DH),
        wi[:, v0:],
        wi[:, z0:a0], jnp.zeros((D, LANES - 2 * GLA_RANK), F32)], axis=1).astype(BF16)
    gw = GLA_HEADS * GLA_DK
    w2 = jnp.zeros((LANES, 2 * gw), F32)
    w2 = w2.at[0:GLA_RANK, 0:gw].set(w_gate2_f[0]).at[GLA_RANK:2 * GLA_RANK, gw:].set(w_gate2_b[0]).astype(BF16)
    bg = jnp.concatenate([b_gate_f[0], b_gate_b[0]])[None]
    cos, sin = _rope_tables(T)
    ones = jnp.ones((L, LANES), F32)

    gq, gk, gv, r, g, aq, ak, av = _inproj(x, mod3, lambda b: b * N_MOD + 1, lambda b: b * N_MOD,
                                           w_p, w2, bg, cos, sin, tm=min(T, INPROJ_TOKENS))
    _, cgk, cgv, _, cg, _, cak, cav = _inproj(ctx, mod3, lambda b: B * N_MOD + 1, lambda b: B * N_MOD,
                                              w_p, w2, bg, ones, 0.0 * ones, tm=L)

    of, ob = _gla(gq, gk, g, gv, cgk, cg, cgv, tt=min(T, GLA_TOKENS))
    sink_bc = jnp.broadcast_to((attn_sink[0] * LOG2E)[:, None], (ATT_HEADS, LANES))
    att = _attn(aq, ak, av, cak, cav, sink_bc)

    sk = peer_subkeys[0]
    half = PEER_DQ // 2
    skb = jnp.zeros((PEER_HEADS, 2 * N_KEYS, PEER_DQ), F32)
    skb = skb.at[:, :N_KEYS, :half].set(sk[:, 0]).at[:, N_KEYS:, half:].set(sk[:, 1])
    skb = skb.astype(BF16)
    x1, hpt, st = _outproj(of, ob, r, att, x, mod3, w_out[0].astype(BF16), gla_norm_g[0][None],
                           ln1_g[0][None], ln1_b[0][None], peer_wq[0].T.astype(BF16), skb,
                           alpha, tm=min(T, OUTPROJ_TOKENS))

    g, kap = _topk(st, tp=min(T, TOPK_TOKENS))
    u, vt = _expert_tables(peer_u[0], peer_v[0], te=PEER_EXPERTS)
    return _peer(hpt, u, vt, st, g, kap, x1, mod3, ln2_g[0][None], ln2_b[0][None],
                 alpha, tm=min(T, PEER_TOKENS))
```

```python
import functools

import jax
import jax.numpy as jnp
from jax import lax
from jax.experimental import pallas as pl
from jax.experimental.pallas import tpu as pltpu

F32 = jnp.float32
BF16 = jnp.bfloat16

GRID_W = 64
GLA_HEADS = 4
GLA_DK = 64
GLA_DV = 128
GLA_RANK = 16
GLA_NORMALIZER = 16.0
GLA_CHUNK = 64
ATT_HEADS = 8
ATT_KV_HEADS = 2
ATT_DH = 64
WINDOW = 128
ATT_BLOCK = 128
ROPE_THETA = 10000.0
PEER_HEADS = 8
N_KEYS = 128
PEER_DQ = 128
PEER_TOPK = 16
N_MOD = 6
LN_EPS = 1e-5
NEG_INF = -1e30
LANES = 128
PEER_CHUNK = 256
LOG2E = 1.4426950408889634

ADA_COLS = 1024
INPROJ_TOKENS = 1024
GLA_TOKENS = 512
OUTPROJ_TOKENS = 512
TOPK_TOKENS = 2048
PEER_TOKENS = 1024
PEER_EXPERTS = 1024

_GW = GLA_HEADS * GLA_DK
_SEGS = (_GW, _GW, GLA_HEADS * GLA_DV, GLA_HEADS * GLA_DV, ATT_HEADS * ATT_DH,
         2 * ATT_KV_HEADS * ATT_DH, ATT_KV_HEADS * ATT_DH, LANES)
_OFFS = tuple(sum(_SEGS[:i]) for i in range(len(_SEGS) + 1))
IN_COLS = _OFFS[-1]
assert all(o % LANES == 0 for o in _OFFS)

VMEM_LIMIT = 56 * 1024 * 1024
PEER_VMEM_LIMIT = 60 * 1024 * 1024

_NT = (((1,), (1,)), ((), ()))
_TN = (((0,), (0,)), ((), ()))


def _params(sem, vmem_limit=VMEM_LIMIT):
    return pltpu.CompilerParams(dimension_semantics=sem, vmem_limit_bytes=vmem_limit)


def _ada_kernel(s_ref, w_ref, b_ref, o_ref, *, n_rows):
    tn = w_ref.shape[1]
    w = w_ref[...]
    rows = []
    for r in range(n_rows):
        s = s_ref[r]
        cols = [jnp.sum(w[:, j * LANES:(j + 1) * LANES] * s, axis=0, keepdims=True)
                for j in range(tn // LANES)]
        rows.append(jnp.concatenate(cols, axis=1) + b_ref[...])
    rows.append(jnp.zeros((8 - n_rows, tn), F32))
    o_ref[...] = jnp.concatenate(rows, axis=0)


def _ada(cond_bc, w, b):
    n_rows, d, _ = cond_bc.shape
    n = w.shape[1]
    tn = ADA_COLS
    return pl.pallas_call(
        functools.partial(_ada_kernel, n_rows=n_rows),
        grid=(n // tn,),
        in_specs=[pl.BlockSpec((n_rows, d, LANES), lambda j: (0, 0, 0)),
                  pl.BlockSpec((d, tn), lambda j: (0, j)),
                  pl.BlockSpec((1, tn), lambda j: (0, j))],
        out_specs=pl.BlockSpec((8, tn), lambda j: (0, j)),
        out_shape=jax.ShapeDtypeStruct((8, n), F32),
        compiler_params=_params(("parallel",)),
        name="ada",
    )(cond_bc, w, b)


def _rope(a, cos, sin):
    n = a.shape[1] // LANES
    c = jnp.concatenate([cos] * n, axis=1)
    s = jnp.concatenate([sin] * n, axis=1)
    lane = lax.broadcasted_iota(jnp.int32, a.shape, 1)
    swapped = jnp.where((lane % 32) < 16,
                        pltpu.roll(a, a.shape[1] - 16, 1),
                        pltpu.roll(a, 16, 1))
    return a * c + swapped * s


def _inproj_kernel(x_ref, sc_ref, sh_ref, w_ref, w2_ref, bg_ref, cos_ref, sin_ref,
                   gq_ref, gk_ref, gv_ref, r_ref, g_ref, aq_ref, ak_ref, av_ref):
    h = x_ref[0] * (1.0 + sc_ref[0]) + sh_ref[0]
    y = jnp.dot(h.astype(BF16), w_ref[...], preferred_element_type=F32)
    seg = lambda i: y[:, _OFFS[i]:_OFFS[i + 1]]
    gq_ref[0] = seg(0) * (GLA_DK ** -0.5)
    gk_ref[0] = seg(1)
    gv_ref[0] = seg(2).astype(BF16)
    r_ref[0] = seg(3)
    cos = cos_ref[...]
    sin = sin_ref[...]
    aq_ref[0] = (_rope(seg(4), cos, sin) * (ATT_DH ** -0.5 * LOG2E)).astype(BF16)
    ak_ref[0] = _rope(seg(5), cos, sin).astype(BF16)
    av_ref[0] = seg(6).astype(BF16)
    z = seg(7)
    logits = jnp.dot(z.astype(BF16), w2_ref[...], preferred_element_type=F32) + bg_ref[...]
    g = (jnp.minimum(logits, 0.0) - jnp.log1p(jnp.exp(-jnp.abs(logits)))) * (1.0 / GLA_NORMALIZER)
    g_ref[0, 0] = g[:, :_GW]
    g_ref[1, 0] = g[:, _GW:]


def _inproj(x, mod3, sc_row, sh_row, w_p, w2, bg, cos, sin, tm):
    B, T, D = x.shape
    NP = w_p.shape[1]
    tok = lambda w, dt: jax.ShapeDtypeStruct((B, T, w), dt)
    tspec = lambda w: pl.BlockSpec((1, tm, w), lambda b, t: (b, t, 0))
    return pl.pallas_call(
        _inproj_kernel,
        grid=(B, T // tm),
        in_specs=[tspec(D),
                  pl.BlockSpec((1, 1, D), lambda b, t: (sc_row(b), 0, 0)),
                  pl.BlockSpec((1, 1, D), lambda b, t: (sh_row(b), 0, 0)),
                  pl.BlockSpec((D, NP), lambda b, t: (0, 0)),
                  pl.BlockSpec((LANES, 2 * _GW), lambda b, t: (0, 0)),
                  pl.BlockSpec((1, 2 * _GW), lambda b, t: (0, 0)),
                  pl.BlockSpec((tm, LANES), lambda b, t: (t, 0)),
                  pl.BlockSpec((tm, LANES), lambda b, t: (t, 0))],
        out_specs=[tspec(_SEGS[0]), tspec(_SEGS[1]), tspec(_SEGS[2]), tspec(_SEGS[3]),
                   pl.BlockSpec((2, 1, tm, _GW), lambda b, t: (0, b, t, 0)),
                   tspec(_SEGS[4]), tspec(_SEGS[5]), tspec(_SEGS[6])],
        out_shape=[tok(_SEGS[0], F32), tok(_SEGS[1], F32), tok(_SEGS[2], BF16), tok(_SEGS[3], F32),
                   jax.ShapeDtypeStruct((2, B, T, _GW), F32),
                   tok(_SEGS[4], BF16), tok(_SEGS[5], BF16), tok(_SEGS[6], BF16)],
        compiler_params=_params(("parallel", "parallel")),
        name="inproj",
    )(x, mod3, mod3, w_p, w2, bg, cos, sin)


def _gla_kernel(qf_ref, kf_ref, gf_ref, vf_ref, qb_ref, kb_ref, gb_ref, vb_ref, ck_ref, cg_ref, cv_ref,
                of_ref, ob_ref, st_ref, *, n_chunks, n_ctx_chunks):
    t = pl.program_id(0)
    nb = qf_ref.shape[0]
    C = GLA_CHUNK
    npairs = GLA_HEADS // 2
    ri = lax.broadcasted_iota(jnp.int32, (C, C), 0)
    ci = lax.broadcasted_iota(jnp.int32, (C, C), 1)
    tri = [ri >= ci, ri <= ci]
    tri_bf = [m.astype(BF16) for m in tri]
    tri2 = [jnp.concatenate([m, m], axis=0) for m in tri]
    lane = lax.broadcasted_iota(jnp.int32, (1, LANES), 1)
    m0 = (lane < GLA_DK).astype(F32)
    m1 = 1.0 - m0
    rr = lax.broadcasted_iota(jnp.int32, (2 * GLA_DV, LANES), 0)
    ll = lax.broadcasted_iota(jnp.int32, (2 * GLA_DV, LANES), 1)
    blockdiag = (rr < GLA_DV) == (ll < GLA_DK)

    def chunks(items, with_out):
        bs = []
        for (_, d, _), _, gc, _, _ in items:
            ghi = gc.astype(BF16)
            glo = (gc - ghi.astype(F32)).astype(BF16)
            bs.append(jnp.dot(tri_bf[d], ghi, preferred_element_type=F32)
                      + jnp.dot(tri_bf[d], glo, preferred_element_type=F32))
        btots = [jnp.sum(it[2], axis=0, keepdims=True) for it in items]
        sts = [st_ref[it[0]] for it in items]
        outs = [None] * len(items)
        if with_out:
            qes = [it[4] * jnp.exp(b) for it, b in zip(items, bs)]
            amats = []
            for it, b, qe in zip(items, bs, qes):
                ke = (it[1] * jnp.exp(-b)).astype(BF16)
                lhs = jnp.concatenate([qe * m0, qe * m1], axis=0).astype(BF16)
                amats.append(lax.dot_general(lhs, ke, _NT, preferred_element_type=F32))
            inters = [lax.dot_general(qe.astype(BF16), st.astype(BF16), _NT, preferred_element_type=F32)
                      for qe, st in zip(qes, sts)]
            for n, (it, a) in enumerate(zip(items, amats)):
                a = jnp.where(tri2[it[0][1]], a, 0.0).astype(BF16)
                oi = jnp.dot(a, it[3], preferred_element_type=F32)
                outs[n] = jnp.concatenate([oi[:C, :GLA_DV], oi[C:, GLA_DV:]], axis=1) + inters[n]
        for it, b, btot, st in zip(items, bs, btots, sts):
            kd = (it[1] * jnp.exp(btot - b)).astype(BF16)
            mt = lax.dot_general(it[3], kd, _TN, preferred_element_type=F32)
            st_ref[it[0]] = st * jnp.exp(btot) + jnp.where(blockdiag, mt, 0.0)
        return outs

    def qk_lanes(p):
        return slice(p * LANES, (p + 1) * LANES)

    def v_lanes(p):
        return slice(p * 2 * GLA_DV, (p + 1) * 2 * GLA_DV)

    @pl.when(t == 0)
    def _():
        st_ref[...] = jnp.zeros_like(st_ref)
        for j in range(n_ctx_chunks):
            items = []
            for bi in range(nb):
                for d in range(2):
                    cj = j if d == 0 else n_ctx_chunks - 1 - j
                    rows = slice(cj * C, (cj + 1) * C)
                    items += [((bi, d, p), ck_ref[bi, rows, qk_lanes(p)], cg_ref[d, bi, rows, qk_lanes(p)],
                               cv_ref[bi, rows, v_lanes(p)], None) for p in range(npairs)]
            chunks(items, False)

    streams = [(qf_ref, kf_ref, gf_ref, vf_ref, of_ref), (qb_ref, kb_ref, gb_ref, vb_ref, ob_ref)]

    def body(j, carry):
        items, dests = [], []
        for bi in range(nb):
            for d, (q_ref, k_ref, g_ref, v_ref, o_ref) in enumerate(streams):
                cj = j if d == 0 else n_chunks - 1 - j
                rows = pl.ds(pl.multiple_of(cj * C, C), C)
                for p in range(npairs):
                    items.append(((bi, d, p), k_ref[bi, rows, qk_lanes(p)], g_ref[0, bi, rows, qk_lanes(p)],
                                  v_ref[bi, rows, v_lanes(p)], q_ref[bi, rows, qk_lanes(p)]))
                    dests.append((o_ref, bi, rows, v_lanes(p)))
        for (o_ref, bi, rows, lanes), out in zip(dests, chunks(items, True)):
            o_ref[bi, rows, lanes] = out
        return carry

    lax.fori_loop(0, n_chunks, body, 0)


def _gla(gq, gk, g, gv, ck, cg, cv, tt):
    B, T, W = gq.shape
    L = ck.shape[1]
    nT = T // tt
    VW = gv.shape[2]
    fwd = lambda t: (0, t, 0)
    bwd = lambda t: (0, nT - 1 - t, 0)
    ctx = lambda t: (0, 0, 0)
    out = jax.ShapeDtypeStruct((B, T, VW), F32)
    return pl.pallas_call(
        functools.partial(_gla_kernel, n_chunks=tt // GLA_CHUNK, n_ctx_chunks=L // GLA_CHUNK),
        grid=(nT,),
        in_specs=[pl.BlockSpec((B, tt, W), fwd), pl.BlockSpec((B, tt, W), fwd),
                  pl.BlockSpec((1, B, tt, W), lambda t: (0, 0, t, 0)),
                  pl.BlockSpec((B, tt, VW), fwd),
                  pl.BlockSpec((B, tt, W), bwd), pl.BlockSpec((B, tt, W), bwd),
                  pl.BlockSpec((1, B, tt, W), lambda t: (1, 0, nT - 1 - t, 0)),
                  pl.BlockSpec((B, tt, VW), bwd),
                  pl.BlockSpec((B, L, W), ctx),
                  pl.BlockSpec((2, B, L, W), lambda t: (0, 0, 0, 0)),
                  pl.BlockSpec((B, L, VW), ctx)],
        out_specs=[pl.BlockSpec((B, tt, VW), fwd), pl.BlockSpec((B, tt, VW), bwd)],
        out_shape=[out, out],
        scratch_shapes=[pltpu.VMEM((B, 2, GLA_HEADS // 2, 2 * GLA_DV, LANES), F32)],
        compiler_params=_params(("arbitrary",)),
        name="gla",
    )(gq, gk, g, gv, gq, gk, g, gv, ck, cg, cv)


def _attn_kernel(q_ref, kp_ref, kc_ref, kn_ref, vp_ref, vc_ref, vn_ref, ck_ref, cv_ref, sink_ref,
                 o_ref, *, n_blocks):
    n = pl.program_id(1)
    BLK = ATT_BLOCK
    L = ck_ref.shape[1]
    nk = 3 * BLK + L
    group = ATT_HEADS // ATT_KV_HEADS
    ki = lax.broadcasted_iota(jnp.int32, (nk, group * BLK), 0)
    qi = lax.broadcasted_iota(jnp.int32, (nk, group * BLK), 1) & (BLK - 1)
    band = jnp.abs(ki - BLK - qi) <= WINDOW
    prev_ok = jnp.logical_or(ki >= BLK, n > 0)
    next_ok = jnp.logical_or(ki < 2 * BLK, n < n_blocks - 1)
    valid = jnp.logical_or(ki >= 3 * BLK, band & prev_ok & next_ok)
    lane = lax.broadcasted_iota(jnp.int32, (BLK, LANES), 1)
    zero = jnp.zeros((), BF16)
    items = range(ATT_HEADS // 2)
    width = 2 * BLK

    def k_rows(gi):
        kv = 2 * gi // group
        ks = slice(kv * LANES, (kv + 1) * LANES)
        return jnp.concatenate([r[0, :, ks] for r in (kp_ref, kc_ref, kn_ref, ck_ref)], axis=0)

    v_rows = jnp.concatenate([r[0] for r in (vp_ref, vc_ref, vn_ref, cv_ref)], axis=0)

    ok = valid[:, :width]
    scores, snks = [], []
    for gi in items:
        q2 = q_ref[0, :, gi * LANES:(gi + 1) * LANES]
        qstack = jnp.concatenate([jnp.where(lane < ATT_DH, q2, zero), jnp.where(lane >= ATT_DH, q2, zero)],
                                 axis=0)
        s = lax.dot_general(k_rows(gi), qstack, _NT,
                            preferred_element_type=F32)
        scores.append(jnp.where(ok, s, NEG_INF))
        snks.append(jnp.concatenate([sink_ref[2 * gi + hh:2 * gi + hh + 1, :] for hh in range(2)], axis=1))
    maxes = [jnp.maximum(jnp.max(s, axis=0, keepdims=True), snk) for s, snk in zip(scores, snks)]
    probs = [jnp.exp2(s - m) for s, m in zip(scores, maxes)]
    dens = [jnp.sum(p, axis=0, keepdims=True) + jnp.exp2(snk - m) for p, snk, m in zip(probs, snks, maxes)]
    outs = [lax.dot_general(v_rows, p.astype(BF16), _TN, preferred_element_type=F32)
            for p in probs]
    for gi, ot, den in zip(items, outs, dens):
        kv = 2 * gi // group
        ot = ot[kv * ATT_DH:(kv + 1) * ATT_DH] / den
        pair = jnp.concatenate([ot[:, :BLK], ot[:, BLK:]], axis=0)
        o_ref[0, :, gi * LANES:(gi + 1) * LANES] = pair.T.astype(o_ref.dtype)


def _attn(aq, ak, av, ck, cv, sink_bc):
    B, T, _ = aq.shape
    L = ck.shape[1]
    nb = T // ATT_BLOCK
    kw, vw = ak.shape[2], av.shape[2]
    cur = lambda b, n: (b, n, 0)
    prev = lambda b, n: (b, jnp.maximum(n - 1, 0), 0)
    nxt = lambda b, n: (b, jnp.minimum(n + 1, nb - 1), 0)
    kspec = lambda im: pl.BlockSpec((1, ATT_BLOCK, kw), im)
    vspec = lambda im: pl.BlockSpec((1, ATT_BLOCK, vw), im)
    return pl.pallas_call(
        functools.partial(_attn_kernel, n_blocks=nb),
        grid=(B, nb),
        in_specs=[pl.BlockSpec((1, ATT_BLOCK, ATT_HEADS * ATT_DH), cur),
                  kspec(prev), kspec(cur), kspec(nxt), vspec(prev), vspec(cur), vspec(nxt),
                  pl.BlockSpec((1, L, kw), lambda b, n: (b, 0, 0)),
                  pl.BlockSpec((1, L, vw), lambda b, n: (b, 0, 0)),
                  pl.BlockSpec((ATT_HEADS, LANES), lambda b, n: (0, 0))],
        out_specs=pl.BlockSpec((1, ATT_BLOCK, ATT_HEADS * ATT_DH), cur),
        out_shape=jax.ShapeDtypeStruct((B, T, ATT_HEADS * ATT_DH), BF16),
        compiler_params=_params(("parallel", "parallel")),
        name="attn",
    )(aq, ak, ak, ak, av, av, av, ck, cv, sink_bc)


def _layernorm(v, g, b):
    mu = jnp.mean(v, axis=-1, keepdims=True)
    vc = v - mu
    var = jnp.mean(vc * vc, axis=-1, keepdims=True)
    return vc * lax.rsqrt(var + LN_EPS) * g + b


def _outproj_kernel(of_ref, ob_ref, r_ref, att_ref, x_ref, g1_ref, sh2_ref, sc2_ref,
                    wout_ref, gng_ref, lng_ref, lnb_ref, wqt_ref, sk_ref,
                    x1_ref, hpt_ref, st_ref, *, alpha):
    o = of_ref[0] + ob_ref[0]
    r = r_ref[0]
    gng = gng_ref[...]
    parts = []
    for h in range(GLA_HEADS):
        oh = o[:, h * GLA_DV:(h + 1) * GLA_DV]
        oh = oh * lax.rsqrt(jnp.mean(oh * oh, axis=-1, keepdims=True) + LN_EPS) * gng
        rh = r[:, h * GLA_DV:(h + 1) * GLA_DV]
        parts.append((oh * (rh * jax.nn.sigmoid(rh))).astype(BF16))
    parts.append(att_ref[0])
    cat = jnp.concatenate(parts, axis=1)
    y = jnp.dot(cat, wout_ref[...], preferred_element_type=F32)
    x1 = _layernorm(alpha * x_ref[0] + g1_ref[0] * y, lng_ref[...], lnb_ref[...])
    x1_ref[0] = x1
    hp = x1 * (1.0 + sc2_ref[0]) + sh2_ref[0]
    hpt = hp.T.astype(BF16)
    for c in range(hpt.shape[1] // PEER_CHUNK):
        hpt_ref[0, c] = hpt[:, c * PEER_CHUNK:(c + 1) * PEER_CHUNK]
    qt = jnp.dot(wqt_ref[...], hpt, preferred_element_type=F32)
    for h in range(PEER_HEADS):
        qh = qt[h * PEER_DQ:(h + 1) * PEER_DQ, :].astype(BF16)
        s = jnp.dot(sk_ref[h], qh, preferred_element_type=F32) * LOG2E
        for c in range(s.shape[1] // LANES):
            st_ref[0, h, c] = s[:, c * LANES:(c + 1) * LANES]


def _outproj(of, ob, r, att, x, mod3, w_out, gng, ln_g, ln_b, wqt, skb, alpha, tm):
    B, T, D = x.shape
    W = of.shape[2]
    tok = lambda w: pl.BlockSpec((1, tm, w), lambda b, t: (b, t, 0))
    modrow = lambda j: pl.BlockSpec((1, 1, D), lambda b, t: (b * N_MOD + j, 0, 0))
    full = lambda a: pl.BlockSpec(a.shape, lambda b, t: (0,) * a.ndim)
    return pl.pallas_call(
        functools.partial(_outproj_kernel, alpha=alpha),
        grid=(B, T // tm),
        in_specs=[tok(W), tok(W), tok(W), tok(att.shape[2]), tok(D), modrow(2), modrow(3), modrow(4),
                  full(w_out), full(gng), full(ln_g), full(ln_b), full(wqt), full(skb)],
        out_specs=[tok(D),
                   pl.BlockSpec((1, tm // PEER_CHUNK, D, PEER_CHUNK), lambda b, t: (b, t, 0, 0)),
                   pl.BlockSpec((1, PEER_HEADS, tm // LANES, 2 * N_KEYS, LANES), lambda b, t: (b, 0, t, 0, 0))],
        out_shape=[jax.ShapeDtypeStruct((B, T, D), F32),
                   jax.ShapeDtypeStruct((B, T // PEER_CHUNK, D, PEER_CHUNK), BF16),
                   jax.ShapeDtypeStruct((B, PEER_HEADS, T // LANES, 2 * N_KEYS, LANES), F32)],
        compiler_params=_params(("parallel", "parallel")),
        name="outproj",
    )(of, ob, r, att, x, mod3, mod3, mod3, w_out, gng, ln_g, ln_b, wqt, skb)


def _sort_network(n):
    pairs = []

    def merge(lo, hi, r):
        step = 2 * r
        if step < hi - lo:
            merge(lo, hi, step)
            merge(lo + r, hi, step)
            pairs.extend((i, i + r) for i in range(lo + r, hi - r, step))
        else:
            pairs.append((lo, lo + r))

    def sort(lo, hi):
        if hi - lo >= 1:
            mid = lo + (hi - lo) // 2
            sort(lo, mid)
            sort(mid + 1, hi)
            merge(lo, hi, 1)

    sort(0, n - 1)
    return pairs


_SLABS = 16
_NET = _sort_network(_SLABS)


def _largest_desc(slabs, count):
    s = list(slabs) + [None] * (_SLABS - len(slabs))
    for i, j in _NET:
        if s[j] is None:
            continue
        if s[i] is None:
            s[i], s[j] = s[j], None
        else:
            s[i], s[j] = jnp.maximum(s[i], s[j]), jnp.minimum(s[i], s[j])
    depth = len(slabs)
    assert all(x is not None for x in s[:depth]) and all(x is None for x in s[depth:])
    vals = []
    for r in range(count):
        m = jnp.max(s[0], axis=0, keepdims=True)
        vals.append(m)
        hit = s[0] == m
        for l in range(min(depth, count - r - 1)):
            s[l] = jnp.where(hit, s[l + 1] if l + 1 < depth else NEG_INF, s[l])
    return vals


def _slabs(v):
    return [v[8 * k:8 * (k + 1)] for k in range(v.shape[0] // 8)]


def _topk_kernel(s_ref, g_ref, kap_ref):
    K = PEER_TOPK
    s = jnp.concatenate([s_ref[0, 0, c] for c in range(s_ref.shape[2])], axis=1)
    s1 = s[0:N_KEYS]
    s2 = s[N_KEYS:2 * N_KEYS]
    a = _largest_desc(_slabs(s1), K + 1)
    b = _largest_desc(_slabs(s2), K + 1)
    bmat = jnp.concatenate(b[:K], axis=0)
    row8 = lax.broadcasted_iota(jnp.int32, (8, s1.shape[1]), 0)
    cands = _slabs(a[0] + bmat)
    for r in range(1, K // 2):
        cands.append(jnp.where(row8 < (K + 1) // (r + 1), a[r] + bmat[0:8], NEG_INF))
    cands.append(jnp.concatenate(a[K // 2:K], axis=0) + b[0])
    cands.append(jnp.where(row8 == 0, a[0] + b[K], jnp.where(row8 == 1, a[K] + b[0], NEG_INF)))
    top = _largest_desc(cands, K + 1)
    tau = 0.5 * (top[K - 1] + top[K])
    mx = top[0]
    zsum = top[0] * 0.0
    for r in range(K):
        zsum = zsum + jnp.exp2(top[r] - mx)
    shift = mx + jnp.log2(zsum)
    g = s1 - shift
    kap = tau - shift
    for ci in range(s1.shape[1] // LANES):
        cols = slice(ci * LANES, (ci + 1) * LANES)
        g_ref[0, 0, ci] = g[:, cols]
        kap_ref[0, 0, ci] = kap[:, cols]


def _topk(st, tp):
    B, H, nchunks, _, _ = st.shape
    out = jax.ShapeDtypeStruct((B, H, nchunks, N_KEYS, LANES), F32)
    ospec = pl.BlockSpec((1, 1, tp // LANES, N_KEYS, LANES), lambda b, h, t: (b, h, t, 0, 0))
    return pl.pallas_call(
        _topk_kernel,
        grid=(B, H, nchunks * LANES // tp),
        in_specs=[pl.BlockSpec((1, 1, tp // LANES, 2 * N_KEYS, LANES), lambda b, h, t: (b, h, t, 0, 0))],
        out_specs=[ospec,
                   pl.BlockSpec((1, 1, tp // LANES, 1, LANES), lambda b, h, t: (b, h, t, 0, 0))],
        out_shape=[out, jax.ShapeDtypeStruct((B, H, nchunks, 1, LANES), F32)],
        compiler_params=_params(("parallel", "parallel", "parallel")),
        name="topk",
    )(st)


def _expert_tables_kernel(u_ref, v_ref, ub_ref, vt_ref):
    ub_ref[...] = u_ref[...].astype(BF16)
    vt_ref[0] = v_ref[...].T.astype(BF16)


def _expert_tables(u, v, te):
    E, D = u.shape
    rows = pl.BlockSpec((te, D), lambda e: (e, 0))
    return pl.pallas_call(
        _expert_tables_kernel,
        grid=(E // te,),
        in_specs=[rows, rows],
        out_specs=[rows, pl.BlockSpec((1, D, te), lambda e: (e, 0, 0))],
        out_shape=[jax.ShapeDtypeStruct((E, D), BF16), jax.ShapeDtypeStruct((E // te, D, te), BF16)],
        compiler_params=_params(("parallel",)),
        name="expert_tables",
    )(u, v)


def _peer_kernel(hpt_ref, u_ref, un_ref, vt_ref, vp_ref, s2_ref, g_ref, kap_ref, x1_ref, g2_ref,
                 lng_ref, lnb_ref, o_ref, acc_ref, z0_ref, z1_ref, a0_ref, a1_ref, *, alpha, n_steps):
    e = pl.program_id(2)
    n_chunks = acc_ref.shape[0]
    te = u_ref.shape[0]
    n_i = te // N_KEYS
    sub = PEER_CHUNK // LANES
    upper = (e * n_i) % 8 >= n_i

    @pl.when(e == 0)
    def _():
        acc_ref[...] = jnp.zeros_like(acc_ref)
        a1_ref[...] = jnp.zeros_like(a1_ref)
        z0_ref[...] = jnp.dot(u_ref[...], hpt_ref[0, 0], preferred_element_type=F32)

    def chunk(k, z_cur, a_cur, z_next, a_prev):
        def next_scores():
            if k + 1 < n_chunks:
                z_next[...] = jnp.dot(u_ref[...], hpt_ref[0, k + 1], preferred_element_type=F32)
            else:
                z_next[...] = jnp.dot(un_ref[...], hpt_ref[0, 0], preferred_element_type=F32)

        def prev_output():
            if k > 0:
                acc_ref[k - 1] += jnp.dot(vt_ref[0], a_prev[...], preferred_element_type=F32)
            else:
                acc_ref[n_chunks - 1] += jnp.dot(vp_ref[0], a_prev[...], preferred_element_type=F32)

        def tile(ii, cc):
            rows = slice(ii * N_KEYS, (ii + 1) * N_KEYS)
            cols = slice(cc * LANES, (cc + 1) * LANES)
            c = k * sub + cc
            w = None
            for h in range(PEER_HEADS):
                grow = g_ref[0, h, c, ii:ii + 1, :]
                if 2 * n_i <= 8:
                    grow = jnp.where(upper, g_ref[0, h, c, n_i + ii:n_i + ii + 1, :], grow)
                t = s2_ref[0, h, c] + grow
                gate = jnp.exp2(jnp.where(t >= kap_ref[0, h, c], t, NEG_INF))
                w = gate if w is None else w + gate
            zt = z_cur[rows, cols]
            act = zt * (1.0 + lax.erf(zt * (2.0 ** -0.5)))
            a_cur[rows, cols] = (w * act).astype(BF16)

        tiles = [(ii, cc) for ii in range(n_i) for cc in range(sub)]
        prev_output()
        for ii, cc in tiles[:len(tiles) // 2]:
            tile(ii, cc)
        next_scores()
        for ii, cc in tiles[len(tiles) // 2:]:
            tile(ii, cc)

    for k in range(n_chunks):
        if k % 2 == 0:
            chunk(k, z0_ref, a0_ref, z1_ref, a1_ref)
        else:
            chunk(k, z1_ref, a1_ref, z0_ref, a0_ref)

    @pl.when(e == n_steps - 1)
    def _():
        acc_ref[n_chunks - 1] += jnp.dot(vt_ref[0], a1_ref[...], preferred_element_type=F32)
        half_gate = 0.5 * g2_ref[0]
        for k in range(n_chunks):
            rows = slice(k * PEER_CHUNK, (k + 1) * PEER_CHUNK)
            y = acc_ref[k].T
            o_ref[0, rows, :] = _layernorm(alpha * x1_ref[0, rows, :] + half_gate * y,
                                           lng_ref[...], lnb_ref[...])


def _peer(hpt, u, vt, s2, g, kap, x1, mod3, ln_g, ln_b, alpha, tm):
    B, _, D, _ = hpt.shape
    T = x1.shape[1]
    n_steps, _, te = vt.shape
    nc = tm // LANES
    nk = tm // PEER_CHUNK
    side = pl.BlockSpec((1, PEER_HEADS, nc, N_KEYS, LANES), lambda b, t, e: (b, 0, t, 1, 0))
    n_i = te // N_KEYS
    assert n_i in (4, 8) and nk % 2 == 0
    keyrows = pl.BlockSpec((1, PEER_HEADS, nc, 8, LANES), lambda b, t, e: (b, 0, t, (e * n_i) // 8, 0))
    return pl.pallas_call(
        functools.partial(_peer_kernel, alpha=alpha, n_steps=n_steps),
        grid=(B, T // tm, n_steps),
        in_specs=[pl.BlockSpec((1, nk, D, PEER_CHUNK), lambda b, t, e: (b, t, 0, 0)),
                  pl.BlockSpec((te, D), lambda b, t, e: (e, 0)),
                  pl.BlockSpec((te, D), lambda b, t, e: (jnp.minimum(e + 1, n_steps - 1), 0)),
                  pl.BlockSpec((1, D, te), lambda b, t, e: (e, 0, 0)),
                  pl.BlockSpec((1, D, te), lambda b, t, e: (jnp.maximum(e - 1, 0), 0, 0)),
                  side, keyrows,
                  pl.BlockSpec((1, PEER_HEADS, nc, 1, LANES), lambda b, t, e: (b, 0, t, 0, 0)),
                  pl.BlockSpec((1, tm, D), lambda b, t, e: (b, t, 0)),
                  pl.BlockSpec((1, 1, D), lambda b, t, e: (b * N_MOD + 5, 0, 0)),
                  pl.BlockSpec((1, D), lambda b, t, e: (0, 0)),
                  pl.BlockSpec((1, D), lambda b, t, e: (0, 0))],
        out_specs=pl.BlockSpec((1, tm, D), lambda b, t, e: (b, t, 0)),
        out_shape=jax.ShapeDtypeStruct((B, T, D), F32),
        scratch_shapes=[pltpu.VMEM((nk, D, PEER_CHUNK), F32),
                        pltpu.VMEM((te, PEER_CHUNK), F32), pltpu.VMEM((te, PEER_CHUNK), F32),
                        pltpu.VMEM((te, PEER_CHUNK), BF16), pltpu.VMEM((te, PEER_CHUNK), BF16)],
        compiler_params=_params(("parallel", "parallel", "arbitrary"), PEER_VMEM_LIMIT),
        name="peer",
    )(hpt, u, u, vt, vt, s2, g, kap, x1, mod3, ln_g, ln_b)


def _rope_tables(n_tokens):
    rows = n_tokens // GRID_W
    row = jnp.repeat(jnp.arange(rows, dtype=F32), GRID_W)
    col = jnp.tile(jnp.arange(GRID_W, dtype=F32), rows)
    n_freq = ATT_DH // 4
    inv = ROPE_THETA ** (-jnp.arange(n_freq, dtype=F32) / n_freq)
    ar, ac = row[:, None] * inv, col[:, None] * inv
    cos = jnp.concatenate([jnp.cos(ar), jnp.cos(ar), jnp.cos(ac), jnp.cos(ac)], axis=1)
    sin = jnp.concatenate([-jnp.sin(ar), jnp.sin(ar), -jnp.sin(ac), jnp.sin(ac)], axis=1)
    return jnp.tile(cos, (1, 2)), jnp.tile(sin, (1, 2))


def _dup_heads(w, n_heads, dh):
    return jnp.concatenate([w[:, h * dh:(h + 1) * dh] for h in range(n_heads) for _ in range(2)], axis=1)


def kernel(x, c, ctx, c_ctx, w_ada, b_ada, w_in, w_gate2_f, b_gate_f, w_gate2_b, b_gate_b, gla_norm_g,
           attn_sink, w_out, ln1_g, ln1_b, peer_wq, peer_subkeys, peer_u, peer_v, ln2_g, ln2_b):
    B, T, D = x.shape
    L = ctx.shape[1]
    depth = w_ada.shape[0]
    assert depth == 1, "single-layer block only"
    alpha = (2.0 * depth) ** 0.25

    cond = jnp.concatenate([jax.nn.silu(c), jax.nn.silu(c_ctx)[None]], axis=0)
    cond_bc = jnp.broadcast_to(cond[:, :, None], (B + 1, D, LANES))
    mod = _ada(cond_bc, w_ada[0], b_ada[0][None])
    mod3 = mod.reshape(8 * N_MOD, 1, D)

    wi = w_in[0]
    kq = 2 * GLA_HEADS * GLA_DK + 2 * GLA_HEADS * GLA_DV
    z0, a0 = kq, kq + 2 * GLA_RANK
    k0 = a0 + ATT_HEADS * ATT_DH
    v0 = k0 + ATT_KV_HEADS * ATT_DH
    w_p = jnp.concatenate([
        wi[:, :kq], wi[:, a0:k0],
        _dup_heads(wi[:, k0:v0], ATT_KV_HEADS, ATT_DH),
        wi[:, v0:],
        wi[:, z0:a0], jnp.zeros((D, LANES - 2 * GLA_RANK), F32)], axis=1).astype(BF16)
    gw = GLA_HEADS * GLA_DK
    w2 = jnp.zeros((LANES, 2 * gw), F32)
    w2 = w2.at[0:GLA_RANK, 0:gw].set(w_gate2_f[0]).at[GLA_RANK:2 * GLA_RANK, gw:].set(w_gate2_b[0]).astype(BF16)
    bg = jnp.concatenate([b_gate_f[0], b_gate_b[0]])[None]
    cos, sin = _rope_tables(T)
    ones = jnp.ones((L, LANES), F32)

    gq, gk, gv, r, g, aq, ak, av = _inproj(x, mod3, lambda b: b * N_MOD + 1, lambda b: b * N_MOD,
                                           w_p, w2, bg, cos, sin, tm=min(T, INPROJ_TOKENS))
    _, cgk, cgv, _, cg, _, cak, cav = _inproj(ctx, mod3, lambda b: B * N_MOD + 1, lambda b: B * N_MOD,
                                              w_p, w2, bg, ones, 0.0 * ones, tm=L)

    of, ob = _gla(gq, gk, g, gv, cgk, cg, cgv, tt=min(T, GLA_TOKENS))
    sink_bc = jnp.broadcast_to((attn_sink[0] * LOG2E)[:, None], (ATT_HEADS, LANES))
    att = _attn(aq, ak, av, cak, cav, sink_bc)

    sk = peer_subkeys[0]
    half = PEER_DQ // 2
    skb = jnp.zeros((PEER_HEADS, 2 * N_KEYS, PEER_DQ), F32)
    skb = skb.at[:, :N_KEYS, :half].set(sk[:, 0]).at[:, N_KEYS:, half:].set(sk[:, 1])
    skb = skb.astype(BF16)
    x1, hpt, st = _outproj(of, ob, r, att, x, mod3, w_out[0].astype(BF16), gla_norm_g[0][None],
                           ln1_g[0][None], ln1_b[0][None], peer_wq[0].T.astype(BF16), skb,
                           alpha, tm=min(T, OUTPROJ_TOKENS))

    g, kap = _topk(st, tp=min(T, TOPK_TOKENS))
    u, vt = _expert_tables(peer_u[0], peer_v[0], te=PEER_EXPERTS)
    return _peer(hpt, u, vt, st, g, kap, x1, mod3, ln2_g[0][None], ln2_b[0][None],
                 alpha, tm=min(T, PEER_TOKENS))
```
